```python
import jax, jax.numpy as jnp
from jax import lax
import numpy as np

D_MODEL = 4096
BATCH = 4
SEQ = 2048
DEPTH = 4
DEC_BATCH = 128
DEC_SEQ = 8
PAST_LEN = 16384
PAGE_SIZE = 128

D_RNN = D_MODEL
RG_HEADS = 16
RG_BLK = D_RNN // RG_HEADS
RG_C = 8.0
CONV4_W = 4
D_CONV = D_MODEL // 2
SC_GROUPS = 16
SC_W = 3
D_FF = 4 * D_MODEL
IN_COLS = 2 * D_RNN + 3 * D_CONV + 2 * D_MODEL
EPS = 1e-6

kernel_name = "griffin_shortconv_hybrid_step"


def rms_norm(x, g):
    xf = x.astype(jnp.float32)
    y = xf * lax.rsqrt(jnp.mean(xf * xf, axis=-1, keepdims=True) + EPS)
    return (y * g.astype(jnp.float32)).astype(x.dtype)


def causal_dwconv(x, buf, w):
    width = w.shape[0]
    s = x.shape[1]
    xx = jnp.concatenate([buf.astype(x.dtype), x], axis=1)
    y = xx[:, 0:s] * w[0]
    for k in range(1, width):
        y = y + xx[:, k:k + s] * w[k]
    return y, xx[:, s:]


def block_diag(x, w, b):
    bsz, s, _ = x.shape
    xh = x.reshape(bsz, s, RG_HEADS, RG_BLK)
    y = jnp.einsum('bshi,hij->bshj', xh, w).reshape(bsz, s, D_RNN)
    return y + b


def rglru(xc, w_r, b_r, w_i, b_i, lam, h0):
    r = jax.nn.sigmoid(block_diag(xc, w_r, b_r).astype(jnp.float32))
    i = jax.nn.sigmoid(block_diag(xc, w_i, b_i).astype(jnp.float32))
    log_a = -RG_C * r * jax.nn.softplus(-lam.astype(jnp.float32))
    a = jnp.exp(log_a)
    mult = jnp.sqrt(-jnp.expm1(2.0 * log_a))
    bterm = mult * (i * xc.astype(jnp.float32))

    def step(h, ab):
        a_t, b_t = ab
        h = a_t * h + b_t
        return h, h

    h_last, hs = lax.scan(step, h0.astype(jnp.float32),
                          (jnp.swapaxes(a, 0, 1), jnp.swapaxes(bterm, 0, 1)))
    return jnp.swapaxes(hs, 0, 1).astype(xc.dtype), h_last.astype(h0.dtype)


def trunk_layer(x, h0, c4_buf, sc_buf, norm_mix_pre, norm_mix_post, w_in, w_conv4, b_conv4,
                w_rg_r, b_rg_r, w_rg_i, b_rg_i, rg_lambda, w_out_a, w_sc, w_out_b, w_o,
                norm_mlp_pre, norm_mlp_post, w_mlp_up, w_mlp_down):
    h = rms_norm(x, norm_mix_pre)
    z = h @ w_in
    o = 0
    rnn_x = z[..., o:o + D_RNN]; o += D_RNN
    rnn_g = z[..., o:o + D_RNN]; o += D_RNN
    sc_b = z[..., o:o + D_CONV]; o += D_CONV
    sc_c = z[..., o:o + D_CONV]; o += D_CONV
    sc_x = z[..., o:o + D_CONV]; o += D_CONV
    gate_a = z[..., o:o + D_MODEL]; o += D_MODEL
    gate_b = z[..., o:o + D_MODEL]

    xc, c4_new = causal_dwconv(rnn_x, c4_buf, w_conv4)
    xc = xc + b_conv4
    y_rnn, h_new = rglru(xc, w_rg_r, b_rg_r, w_rg_i, b_rg_i, rg_lambda, h0)
    y_a = (jax.nn.gelu(rnn_g) * y_rnn) @ w_out_a

    u = sc_c * sc_x
    uc, sc_new = causal_dwconv(u, sc_buf, w_sc)
    y_b = (sc_b * uc) @ w_out_b

    m = jax.nn.sigmoid(gate_a) * y_a + jax.nn.sigmoid(gate_b) * y_b
    x = x + rms_norm(m @ w_o, norm_mix_post)

    hm = rms_norm(x, norm_mlp_pre)
    f = jnp.square(jax.nn.relu(hm @ w_mlp_up)) @ w_mlp_down
    x = x + rms_norm(f, norm_mlp_post)
    return x, h_new, c4_new, sc_new


def setup_inputs(seed: int = 0) -> dict:
    key = jax.random.key(seed)
    ks = jax.random.split(key, 32)
    f32 = jnp.float32

    def nrm(k, shape, scale):
        return jax.random.normal(k, shape, f32) * scale

    u = jax.random.uniform(ks[15], (DEPTH, D_RNN), f32, 0.9, 0.999)
    p = u ** (1.0 / RG_C)
    rg_lambda = jnp.log(p) - jnp.log1p(-p)

    return {
        "x_prompt": nrm(ks[0], (BATCH, SEQ, D_MODEL), 1.0),
        "x_sample": nrm(ks[1], (DEC_BATCH, DEC_SEQ, D_MODEL), 1.0),
        "state_rglru_h": nrm(ks[2], (DEPTH, DEC_BATCH, D_RNN), 0.5),
        "state_conv4": nrm(ks[3], (DEPTH, DEC_BATCH, CONV4_W - 1, D_RNN), 1.0),
        "state_shortconv": nrm(ks[4], (DEPTH, DEC_BATCH, SC_W - 1, D_CONV), 1.0),
        "norm_mix_pre": 1.0 + nrm(ks[5], (DEPTH, D_MODEL), 0.05),
        "norm_mix_post": 1.0 + nrm(ks[6], (DEPTH, D_MODEL), 0.05),
        "w_in": nrm(ks[7], (DEPTH, D_MODEL, IN_COLS), D_MODEL ** -0.5),
        "w_conv4": nrm(ks[8], (DEPTH, CONV4_W, D_RNN), CONV4_W ** -0.5),
        "b_conv4": nrm(ks[9], (DEPTH, D_RNN), 0.05),
        "w_rg_r": nrm(ks[10], (DEPTH, RG_HEADS, RG_BLK, RG_BLK), RG_BLK ** -0.5),
        "b_rg_r": nrm(ks[11], (DEPTH, D_RNN), 0.05),
        "w_rg_i": nrm(ks[12], (DEPTH, RG_HEADS, RG_BLK, RG_BLK), RG_BLK ** -0.5),
        "b_rg_i": nrm(ks[13], (DEPTH, D_RNN), 0.05),
        "rg_lambda": rg_lambda,
        "w_out_a": nrm(ks[16], (DEPTH, D_RNN, D_MODEL), D_RNN ** -0.5),
        "w_sc": nrm(ks[17], (DEPTH, SC_W, D_CONV), SC_W ** -0.5),
        "w_out_b": nrm(ks[18], (DEPTH, D_CONV, D_MODEL), D_CONV ** -0.5),
        "w_o": nrm(ks[19], (DEPTH, D_MODEL, D_MODEL), D_MODEL ** -0.5),
        "norm_mlp_pre": 1.0 + nrm(ks[20], (DEPTH, D_MODEL), 0.05),
        "norm_mlp_post": 1.0 + nrm(ks[21], (DEPTH, D_MODEL), 0.05),
        "w_mlp_up": nrm(ks[22], (DEPTH, D_MODEL, D_FF), D_MODEL ** -0.5),
        "w_mlp_down": nrm(ks[23], (DEPTH, D_FF, D_MODEL), D_FF ** -0.5),
    }


def reference(x_prompt, x_sample, state_rglru_h, state_conv4, state_shortconv,
              norm_mix_pre, norm_mix_post, w_in, w_conv4, b_conv4, w_rg_r, b_rg_r,
              w_rg_i, b_rg_i, rg_lambda, w_out_a, w_sc, w_out_b, w_o,
              norm_mlp_pre, norm_mlp_post, w_mlp_up, w_mlp_down):
    bp = x_prompt.shape[0]
    dt = x_prompt.dtype
    xp = x_prompt
    xs = x_sample
    hp_l, c4p_l, scp_l = [], [], []
    hs_l, c4s_l, scs_l = [], [], []
    for l in range(DEPTH):
        params = (norm_mix_pre[l], norm_mix_post[l], w_in[l], w_conv4[l], b_conv4[l],
                  w_rg_r[l], b_rg_r[l], w_rg_i[l], b_rg_i[l], rg_lambda[l], w_out_a[l],
                  w_sc[l], w_out_b[l], w_o[l], norm_mlp_pre[l], norm_mlp_post[l],
                  w_mlp_up[l], w_mlp_down[l])
        xp, hp, c4p, scp = trunk_layer(
            xp,
            jnp.zeros((bp, D_RNN), dt),
            jnp.zeros((bp, CONV4_W - 1, D_RNN), dt),
            jnp.zeros((bp, SC_W - 1, D_CONV), dt),
            *params)
        xs, hs, c4s, scs = trunk_layer(xs, state_rglru_h[l], state_conv4[l], state_shortconv[l],
                                       *params)
        hp_l.append(hp); c4p_l.append(c4p); scp_l.append(scp)
        hs_l.append(hs); c4s_l.append(c4s); scs_l.append(scs)
    return (xp, xs,
            jnp.stack(hp_l), jnp.stack(c4p_l), jnp.stack(scp_l),
            jnp.stack(hs_l), jnp.stack(c4s_l), jnp.stack(scs_l))
```

```python
import functools

import jax
import jax.numpy as jnp
from jax import lax
from jax.experimental import pallas as pl
from jax.experimental.pallas import tpu as pltpu

D_MODEL = 4096
D_RNN = 4096
D_CONV = 2048
D_FF = 16384
RG_BLK = 256
RG_C = 8.0
CONV4_W = 4
SC_W = 3
EPS = 1e-6

OFF_RNN_X = 0
OFF_RNN_G = D_RNN
OFF_SC_B = 2 * D_RNN
OFF_SC_C = OFF_SC_B + D_CONV
OFF_SC_X = OFF_SC_C + D_CONV
OFF_GATE_A = OFF_SC_X + D_CONV
OFF_GATE_B = OFF_GATE_A + D_MODEL
IN_COLS = OFF_GATE_B + D_MODEL

SUBLANES = 8
HALO = SUBLANES
V7X_VMEM_BUDGET = 58 * 1024 * 1024

F32 = jnp.float32
BF16 = jnp.bfloat16


def _params(semantics, vmem_bytes):
    return pltpu.CompilerParams(
        dimension_semantics=semantics,
        vmem_limit_bytes=min(int(vmem_bytes) + (12 << 20), V7X_VMEM_BUDGET),
    )


def _rms(x, g):
    ms = jnp.mean(x * x, axis=-1, keepdims=True)
    return (x * lax.rsqrt(ms + EPS)) * g


def _norm_kernel(x_ref, g_ref, o_ref):
    o_ref[...] = _rms(x_ref[...], g_ref[...]).astype(o_ref.dtype)


def _norm(x, g, tm=256):
    m, d = x.shape
    return pl.pallas_call(
        _norm_kernel,
        grid=(m // tm,),
        in_specs=[pl.BlockSpec((tm, d), lambda i: (i, 0)),
                  pl.BlockSpec((1, d), lambda i: (0, 0))],
        out_specs=pl.BlockSpec((tm, d), lambda i: (i, 0)),
        out_shape=jax.ShapeDtypeStruct((m, d), BF16),
        compiler_params=_params(("parallel",), 2 * tm * d * 6),
        name="norm0",
    )(x, g)


def _mm_kernel(x_ref, w_ref, o_ref, *, relu2):
    acc = jnp.dot(x_ref[...], w_ref[...], preferred_element_type=F32)
    if relu2:
        acc = jnp.square(jnp.maximum(acc, 0.0))
    o_ref[...] = acc.astype(o_ref.dtype)


def _mm(x, w, out_dtype, relu2, name, tm=1024, tn=1024):
    m, k = x.shape
    n = w.shape[1]
    vm = 2 * (tm * k * 2 + k * tn * 2 + tm * tn * jnp.dtype(out_dtype).itemsize)
    return pl.pallas_call(
        functools.partial(_mm_kernel, relu2=relu2),
        grid=(m // tm, n // tn),
        in_specs=[pl.BlockSpec((tm, k), lambda i, j: (i, 0)),
                  pl.BlockSpec((k, tn), lambda i, j: (0, j))],
        out_specs=pl.BlockSpec((tm, tn), lambda i, j: (i, j)),
        out_shape=jax.ShapeDtypeStruct((m, n), out_dtype),
        compiler_params=_params(("parallel", "arbitrary"), vm),
        name=name,
    )(x, w)


def _merge_kernel(ua_ref, ub_ref, wa_ref, wb_ref, ga_ref, gb_ref, o_ref):
    ya = jnp.dot(ua_ref[...], wa_ref[...], preferred_element_type=F32)
    yb = jnp.dot(ub_ref[...], wb_ref[...], preferred_element_type=F32)
    m = jax.nn.sigmoid(ga_ref[...]) * ya + jax.nn.sigmoid(gb_ref[...]) * yb
    o_ref[...] = m.astype(o_ref.dtype)


def _merge(ua, ub, wa, wb, z, tm=1024, tn=512):
    m = ua.shape[0]
    vm = 2 * (tm * D_RNN * 2 + tm * D_CONV * 2 + D_RNN * tn * 2 + D_CONV * tn * 2
              + 2 * tm * tn * 4 + tm * tn * 2)
    return pl.pallas_call(
        _merge_kernel,
        grid=(m // tm, D_MODEL // tn),
        in_specs=[pl.BlockSpec((tm, D_RNN), lambda i, j: (i, 0)),
                  pl.BlockSpec((tm, D_CONV), lambda i, j: (i, 0)),
                  pl.BlockSpec((D_RNN, tn), lambda i, j: (0, j)),
                  pl.BlockSpec((D_CONV, tn), lambda i, j: (0, j)),
                  pl.BlockSpec((tm, tn), lambda i, j: (i, OFF_GATE_A // tn + j)),
                  pl.BlockSpec((tm, tn), lambda i, j: (i, OFF_GATE_B // tn + j))],
        out_specs=pl.BlockSpec((tm, tn), lambda i, j: (i, j)),
        out_shape=jax.ShapeDtypeStruct((m, D_MODEL), BF16),
        compiler_params=_params(("parallel", "arbitrary"), vm),
        name="merge",
    )(ua, ub, wa, wb, z, z)


def _mm_norm_kernel(a_ref, w_ref, x_ref, gpost_ref, gnext_ref, xo_ref, ho_ref, acc_ref,
                    *, nk, te):
    k = pl.program_id(1)

    @pl.when(k == 0)
    def _():
        acc_ref[...] = jnp.dot(a_ref[...], w_ref[...], preferred_element_type=F32)

    @pl.when(jnp.logical_and(k > 0, k < nk))
    def _():
        acc_ref[...] += jnp.dot(a_ref[...], w_ref[...], preferred_element_type=F32)

    @pl.when(k >= nk)
    def _():
        r0 = pl.multiple_of((k - nk) * te, te)
        o = acc_ref[pl.ds(r0, te), :]
        x1 = x_ref[...] + _rms(o, gpost_ref[...])
        xo_ref[...] = x1
        ho_ref[...] = _rms(x1, gnext_ref[...]).astype(ho_ref.dtype)


def _mm_norm(a, w, x, gpost, gnext, name, tm=1024, tk=1024, te=128):
    m, kdim = a.shape
    d = w.shape[1]
    nk = kdim // tk
    ne = tm // te
    vm = (2 * (tm * tk * 2 + tk * d * 2 + te * d * 4 + te * d * 4 + te * d * 2)
          + tm * d * 4)

    def slab(i, k):
        return (i * ne + jnp.maximum(k - nk, 0), 0)

    return pl.pallas_call(
        functools.partial(_mm_norm_kernel, nk=nk, te=te),
        grid=(m // tm, nk + ne),
        in_specs=[pl.BlockSpec((tm, tk), lambda i, k: (i, jnp.minimum(k, nk - 1))),
                  pl.BlockSpec((tk, d), lambda i, k: (jnp.minimum(k, nk - 1), 0)),
                  pl.BlockSpec((te, d), slab),
                  pl.BlockSpec((1, d), lambda i, k: (0, 0)),
                  pl.BlockSpec((1, d), lambda i, k: (0, 0))],
        out_specs=[pl.BlockSpec((te, d), slab),
                   pl.BlockSpec((te, d), slab)],
        out_shape=[jax.ShapeDtypeStruct((m, d), F32),
                   jax.ShapeDtypeStruct((m, d), BF16)],
        scratch_shapes=[pltpu.VMEM((tm, d), F32)],
        compiler_params=_params(("parallel", "arbitrary"), vm),
        name=name,
    )(a, w, x, gpost, gnext)


def _rglru_coeffs(xc, wr_ref, wi_ref, br, bi, lam, a_ref, b_ref):
    sp = jnp.maximum(-lam, 0.0) + jnp.log1p(jnp.exp(-jnp.abs(lam)))
    xcb = xc.astype(BF16)
    for g in range(xc.shape[1] // RG_BLK):
        sl = slice(g * RG_BLK, (g + 1) * RG_BLK)
        xg = xcb[:, sl]
        r = jax.nn.sigmoid(jnp.dot(xg, wr_ref[g], preferred_element_type=F32) + br[:, sl])
        i = jax.nn.sigmoid(jnp.dot(xg, wi_ref[g], preferred_element_type=F32) + bi[:, sl])
        log_a = (-RG_C * r) * sp[:, sl]
        a = jnp.exp(log_a)
        mult = jnp.sqrt(-jnp.tanh(log_a) * (a * a + 1.0))
        a_ref[:, sl] = a
        b_ref[:, sl] = mult * (i * xc[:, sl])


def _scan8(a, b):
    row = lax.broadcasted_iota(jnp.int32, a.shape, 0)
    for s in (1, 2, 4):
        keep = row >= s
        b = jnp.where(keep, a * pltpu.roll(b, s, 0) + b, b)
        a = jnp.where(keep, a * pltpu.roll(a, s, 0), a)
    return a, b


def _last_row(h):
    return jnp.broadcast_to(h[SUBLANES - 1:SUBLANES, :], h.shape)


def _mix_a_prompt_kernel(x_ref, g_ref, w4_ref, b4_ref, wr_ref, wi_ref, br_ref, bi_ref,
                         lam_ref, u_ref, hn_ref, cn_ref,
                         xpad_ref, a_ref, b_ref, hs_ref, hcar_ref, *, tt):
    c = pl.program_id(2)
    tc = x_ref.shape[1]

    @pl.when(c == 0)
    def _():
        xpad_ref[0:HALO, :] = jnp.zeros((HALO, tc), F32)
        hcar_ref[...] = jnp.zeros((SUBLANES, tc), F32)

    x = x_ref[...]
    xpad_ref[HALO:HALO + tt, :] = x
    w4 = w4_ref[...]
    xc = xpad_ref[HALO - 3:HALO - 3 + tt, :] * w4[0:1, :]
    xc = xc + xpad_ref[HALO - 2:HALO - 2 + tt, :] * w4[1:2, :]
    xc = xc + xpad_ref[HALO - 1:HALO - 1 + tt, :] * w4[2:3, :]
    xc = xc + x * w4[3:4, :]
    xc = xc + b4_ref[...]
    xpad_ref[0:HALO, :] = x_ref[tt - HALO:tt, :]

    _rglru_coeffs(xc, wr_ref, wi_ref, br_ref[...], bi_ref[...], lam_ref[...], a_ref, b_ref)

    def body(q, hprev):
        r0 = pl.multiple_of(q * SUBLANES, SUBLANES)
        a, b = _scan8(a_ref[pl.ds(r0, SUBLANES), :], b_ref[pl.ds(r0, SUBLANES), :])
        h = a * hprev + b
        hs_ref[pl.ds(r0, SUBLANES), :] = h
        return _last_row(h)

    hlast = lax.fori_loop(0, tt // SUBLANES, body, hcar_ref[...])
    hcar_ref[...] = hlast
    u_ref[...] = (jax.nn.gelu(g_ref[...]) * hs_ref[...]).astype(u_ref.dtype)

    @pl.when(c == pl.num_programs(2) - 1)
    def _():
        hn_ref[0] = hlast[0:1, :]
        cn_ref[0] = x_ref[tt - (CONV4_W - 1):tt, :]


def _mix_a_prompt(z, w4, b4, wr, wi, br, bi, lam, batch, seq, tt=256, tc=1024):
    nc = seq // tt
    nj = D_RNN // tc
    row = lambda j, b, c: (b * nc + c, j)
    vec = pl.BlockSpec((1, tc), lambda j, b, c: (0, j))
    blk = pl.BlockSpec((tc // RG_BLK, RG_BLK, RG_BLK), lambda j, b, c: (j, 0, 0))
    vm = 2 * (2 * tt * tc * 4 + tt * tc * 2 + 2 * (tc // RG_BLK) * RG_BLK * RG_BLK * 2) \
        + 4 * (tt + HALO) * tc * 4
    return pl.pallas_call(
        functools.partial(_mix_a_prompt_kernel, tt=tt),
        grid=(nj, batch, nc),
        in_specs=[pl.BlockSpec((tt, tc), lambda j, b, c: (b * nc + c, OFF_RNN_X // tc + j)),
                  pl.BlockSpec((tt, tc), lambda j, b, c: (b * nc + c, OFF_RNN_G // tc + j)),
                  pl.BlockSpec((CONV4_W, tc), lambda j, b, c: (0, j)),
                  vec, blk, blk, vec, vec, vec],
        out_specs=[pl.BlockSpec((tt, tc), row),
                   pl.BlockSpec((1, 1, tc), lambda j, b, c: (b, 0, j)),
                   pl.BlockSpec((1, CONV4_W - 1, tc), lambda j, b, c: (b, 0, j))],
        out_shape=[jax.ShapeDtypeStruct((batch * seq, D_RNN), BF16),
                   jax.ShapeDtypeStruct((batch, 1, D_RNN), F32),
                   jax.ShapeDtypeStruct((batch, CONV4_W - 1, D_RNN), F32)],
        scratch_shapes=[pltpu.VMEM((tt + HALO, tc), F32),
                        pltpu.VMEM((tt, tc), F32),
                        pltpu.VMEM((tt, tc), F32),
                        pltpu.VMEM((tt, tc), F32),
                        pltpu.VMEM((SUBLANES, tc), F32)],
        compiler_params=_params(("parallel", "arbitrary", "arbitrary"), vm),
        name="mix_a_prompt",
    )(z, z, w4, b4, wr, wi, br, bi, lam)


def _mix_a_sample_kernel(x_ref, g_ref, h0_ref, c4_ref, w4_ref, b4_ref, wr_ref, wi_ref,
                         br_ref, bi_ref, lam_ref, u_ref, hn_ref, cn_ref,
                         pad_ref, xc_ref, a_ref, b_ref, hs_ref, *, nseq, seq):
    w4 = w4_ref[...]
    b4 = b4_ref[...]

    def conv(q, carry):
        r0 = pl.multiple_of(q * seq, seq)
        x = x_ref[pl.ds(r0, seq), :]
        pad_ref[HALO - (CONV4_W - 1):HALO, :] = c4_ref[q]
        pad_ref[HALO:HALO + seq, :] = x
        xc = pad_ref[HALO - 3:HALO - 3 + seq, :] * w4[0:1, :]
        xc = xc + pad_ref[HALO - 2:HALO - 2 + seq, :] * w4[1:2, :]
        xc = xc + pad_ref[HALO - 1:HALO - 1 + seq, :] * w4[2:3, :]
        xc = xc + x * w4[3:4, :]
        xc_ref[pl.ds(r0, seq), :] = xc + b4
        cn_ref[q] = pad_ref[HALO + seq - (CONV4_W - 1):HALO + seq, :]
        return carry

    lax.fori_loop(0, nseq, conv, 0)

    _rglru_coeffs(xc_ref[...], wr_ref, wi_ref, br_ref[...], bi_ref[...], lam_ref[...],
                  a_ref, b_ref)

    def body(q, carry):
        r0 = pl.multiple_of(q * seq, seq)
        a, b = _scan8(a_ref[pl.ds(r0, seq), :], b_ref[pl.ds(r0, seq), :])
        h0 = jnp.broadcast_to(h0_ref[pl.ds(q, 1), :], a.shape)
        h = a * h0 + b
        hs_ref[pl.ds(r0, seq), :] = h
        hn_ref[pl.ds(q, 1), :] = h[seq - 1:seq, :]
        return carry

    lax.fori_loop(0, nseq, body, 0)
    u_ref[...] = (jax.nn.gelu(g_ref[...]) * hs_ref[...]).astype(u_ref.dtype)


def _mix_a_sample(z, h0, c4, w4, b4, wr, wi, br, bi, lam, nseq, seq, tc=512):
    assert seq == SUBLANES
    m = nseq * seq
    nj = D_RNN // tc
    vec = pl.BlockSpec((1, tc), lambda j: (0, j))
    blk = pl.BlockSpec((tc // RG_BLK, RG_BLK, RG_BLK), lambda j: (j, 0, 0))
    vm = 2 * (2 * m * tc * 4 + m * tc * 2 + nseq * tc * 4 * 2 + 2 * nseq * 8 * tc * 4) \
        + 4 * m * tc * 4
    return pl.pallas_call(
        functools.partial(_mix_a_sample_kernel, nseq=nseq, seq=seq),
        grid=(nj,),
        in_specs=[pl.BlockSpec((m, tc), lambda j: (0, OFF_RNN_X // tc + j)),
                  pl.BlockSpec((m, tc), lambda j: (0, OFF_RNN_G // tc + j)),
                  pl.BlockSpec((nseq, tc), lambda j: (0, j)),
                  pl.BlockSpec((nseq, CONV4_W - 1, tc), lambda j: (0, 0, j)),
                  pl.BlockSpec((CONV4_W, tc), lambda j: (0, j)),
                  vec, blk, blk, vec, vec, vec],
        out_specs=[pl.BlockSpec((m, tc), lambda j: (0, j)),
                   pl.BlockSpec((nseq, tc), lambda j: (0, j)),
                   pl.BlockSpec((nseq, CONV4_W - 1, tc), lambda j: (0, 0, j))],
        out_shape=[jax.ShapeDtypeStruct((m, D_RNN), BF16),
                   jax.ShapeDtypeStruct((nseq, D_RNN), F32),
                   jax.ShapeDtypeStruct((nseq, CONV4_W - 1, D_RNN), F32)],
        scratch_shapes=[pltpu.VMEM((2 * HALO, tc), F32),
                        pltpu.VMEM((m, tc), F32),
                        pltpu.VMEM((m, tc), F32),
                        pltpu.VMEM((m, tc), F32),
                        pltpu.VMEM((m, tc), F32)],
        compiler_params=_params(("parallel",), vm),
        name="mix_a_sample",
    )(z, z, h0, c4, w4, b4, wr, wi, br, bi, lam)


def _mix_b_prompt_kernel(sb_ref, sc_ref, sx_ref, w_ref, u_ref, sn_ref, upad_ref, *, tt):
    c = pl.program_id(2)
    tc = sb_ref.shape[1]

    @pl.when(c == 0)
    def _():
        upad_ref[0:HALO, :] = jnp.zeros((HALO, tc), F32)

    u = sc_ref[...] * sx_ref[...]
    upad_ref[HALO:HALO + tt, :] = u
    w = w_ref[...]
    uc = upad_ref[HALO - 2:HALO - 2 + tt, :] * w[0:1, :]
    uc = uc + upad_ref[HALO - 1:HALO - 1 + tt, :] * w[1:2, :]
    uc = uc + u * w[2:3, :]
    u_ref[...] = (sb_ref[...] * uc).astype(u_ref.dtype)
    upad_ref[0:HALO, :] = upad_ref[tt:tt + HALO, :]

    @pl.when(c == pl.num_programs(2) - 1)
    def _():
        sn_ref[0] = upad_ref[HALO - (SC_W - 1):HALO, :]


def _mix_b_prompt(z, w, batch, seq, tt=512, tc=1024):
    nc = seq // tt
    nj = D_CONV // tc
    vm = 2 * (3 * tt * tc * 4 + tt * tc * 2) + (tt + HALO) * tc * 4 + 4 * tt * tc * 4

    def col(off):
        return pl.BlockSpec((tt, tc), lambda j, b, c: (b * nc + c, off // tc + j))

    return pl.pallas_call(
        functools.partial(_mix_b_prompt_kernel, tt=tt),
        grid=(nj, batch, nc),
        in_specs=[col(OFF_SC_B), col(OFF_SC_C), col(OFF_SC_X),
                  pl.BlockSpec((SC_W, tc), lambda j, b, c: (0, j))],
        out_specs=[pl.BlockSpec((tt, tc), lambda j, b, c: (b * nc + c, j)),
                   pl.BlockSpec((1, SC_W - 1, tc), lambda j, b, c: (b, 0, j))],
        out_shape=[jax.ShapeDtypeStruct((batch * seq, D_CONV), BF16),
                   jax.ShapeDtypeStruct((batch, SC_W - 1, D_CONV), F32)],
        scratch_shapes=[pltpu.VMEM((tt + HALO, tc), F32)],
        compiler_params=_params(("parallel", "arbitrary", "arbitrary"), vm),
        name="mix_b_prompt",
    )(z, z, z, w)


def _mix_b_sample_kernel(sb_ref, sc_ref, sx_ref, buf_ref, w_ref, u_ref, sn_ref, pad_ref,
                         *, nseq, seq):
    w = w_ref[...]

    def body(q, carry):
        r0 = pl.multiple_of(q * seq, seq)
        u = sc_ref[pl.ds(r0, seq), :] * sx_ref[pl.ds(r0, seq), :]
        pad_ref[HALO - (SC_W - 1):HALO, :] = buf_ref[q]
        pad_ref[HALO:HALO + seq, :] = u
        uc = pad_ref[HALO - 2:HALO - 2 + seq, :] * w[0:1, :]
        uc = uc + pad_ref[HALO - 1:HALO - 1 + seq, :] * w[1:2, :]
        uc = uc + u * w[2:3, :]
        u_ref[pl.ds(r0, seq), :] = (sb_ref[pl.ds(r0, seq), :] * uc).astype(u_ref.dtype)
        sn_ref[q] = pad_ref[HALO + seq - (SC_W - 1):HALO + seq, :]
        return carry

    lax.fori_loop(0, nseq, body, 0)


def _mix_b_sample(z, buf, w, nseq, seq, tc=1024):
    assert seq == SUBLANES
    m = nseq * seq
    nj = D_CONV // tc
    vm = 2 * (3 * m * tc * 4 + m * tc * 4 + 2 * nseq * 8 * tc * 4)

    def col(off):
        return pl.BlockSpec((m, tc), lambda j: (0, off // tc + j))

    return pl.pallas_call(
        functools.partial(_mix_b_sample_kernel, nseq=nseq, seq=seq),
        grid=(nj,),
        in_specs=[col(OFF_SC_B), col(OFF_SC_C), col(OFF_SC_X),
                  pl.BlockSpec((nseq, SC_W - 1, tc), lambda j: (0, 0, j)),
                  pl.BlockSpec((SC_W, tc), lambda j: (0, j))],
        out_specs=[pl.BlockSpec((m, tc), lambda j: (0, j)),
                   pl.BlockSpec((nseq, SC_W - 1, tc), lambda j: (0, 0, j))],
        out_shape=[jax.ShapeDtypeStruct((m, D_CONV), F32),
                   jax.ShapeDtypeStruct((nseq, SC_W - 1, D_CONV), F32)],
        scratch_shapes=[pltpu.VMEM((2 * HALO, tc), F32)],
        compiler_params=_params(("parallel",), vm),
        name="mix_b_sample",
    )(z, z, z, buf, w)


def _layer(x, h, lw, gnext, group):
    z = _mm(h, lw["w_in"], F32, False, "mm_in")
    mixw = (lw["w_conv4"], lw["b_conv4"], lw["w_rg_r"], lw["w_rg_i"], lw["b_rg_r"],
            lw["b_rg_i"], lw["rg_lambda"])
    if group["kind"] == "prompt":
        ua, hn, cn = _mix_a_prompt(z, *mixw, group["batch"], group["seq"])
        ub, sn = _mix_b_prompt(z, lw["w_sc"], group["batch"], group["seq"])
        hn = hn.reshape(group["batch"], D_RNN)
    else:
        ua, hn, cn = _mix_a_sample(z, group["h0"], group["c4"], *mixw,
                                   group["batch"], group["seq"])
        ub, sn = _mix_b_sample(z, group["sc"], lw["w_sc"], group["batch"], group["seq"])
        ub = ub.astype(BF16)
    m = _merge(ua, ub, lw["w_out_a"], lw["w_out_b"], z)
    x1, hm = _mm_norm(m, lw["w_o"], x, lw["norm_mix_post"], lw["norm_mlp_pre"], "mm_o",
                      tk=1024)
    a = _mm(hm, lw["w_mlp_up"], BF16, True, "mm_up")
    x2, hnext = _mm_norm(a, lw["w_mlp_down"], x1, lw["norm_mlp_post"], gnext, "mm_down",
                         tk=1024)
    return x2, hnext, hn, cn, sn


def kernel(x_prompt, x_sample, state_rglru_h, state_conv4, state_shortconv, norm_mix_pre,
           norm_mix_post, w_in, w_conv4, b_conv4, w_rg_r, b_rg_r, w_rg_i, b_rg_i, rg_lambda,
           w_out_a, w_sc, w_out_b, w_o, norm_mlp_pre, norm_mlp_post, w_mlp_up, w_mlp_down):
    depth = w_in.shape[0]
    bp, sp, d = x_prompt.shape
    bs, ss, _ = x_sample.shape

    xp = x_prompt.reshape(bp * sp, d)
    xs = x_sample.reshape(bs * ss, d)
    hp = _norm(xp, norm_mix_pre[0].reshape(1, d))
    hs = _norm(xs, norm_mix_pre[0].reshape(1, d))

    outs_p, outs_s = [], []
    for l in range(depth):
        lw = {
            "w_in": w_in[l].astype(BF16),
            "w_conv4": w_conv4[l],
            "b_conv4": b_conv4[l].reshape(1, -1),
            "w_rg_r": w_rg_r[l].astype(BF16),
            "w_rg_i": w_rg_i[l].astype(BF16),
            "b_rg_r": b_rg_r[l].reshape(1, -1),
            "b_rg_i": b_rg_i[l].reshape(1, -1),
            "rg_lambda": rg_lambda[l].reshape(1, -1),
            "w_out_a": w_out_a[l].astype(BF16),
            "w_sc": w_sc[l],
            "w_out_b": w_out_b[l].astype(BF16),
            "w_o": w_o[l].astype(BF16),
            "norm_mix_post": norm_mix_post[l].reshape(1, -1),
            "norm_mlp_pre": norm_mlp_pre[l].reshape(1, -1),
            "norm_mlp_post": norm_mlp_post[l].reshape(1, -1),
            "w_mlp_up": w_mlp_up[l].astype(BF16),
            "w_mlp_down": w_mlp_down[l].astype(BF16),
        }
        gnext = norm_mix_pre[(l + 1) % depth].reshape(1, -1)
        gp = {"kind": "prompt", "batch": bp, "seq": sp}
        gs = {"kind": "sample", "batch": bs, "seq": ss, "h0": state_rglru_h[l],
              "c4": state_conv4[l], "sc": state_shortconv[l]}
        xp, hp, hnp, cnp, snp = _layer(xp, hp, lw, gnext, gp)
        xs, hs, hns, cns, sns = _layer(xs, hs, lw, gnext, gs)
        outs_p.append((hnp, cnp, snp))
        outs_s.append((hns, cns, sns))

    return (xp.reshape(bp, sp, d), xs.reshape(bs, ss, d),
            jnp.stack([o[0] for o in outs_p]), jnp.stack([o[1] for o in outs_p]),
            jnp.stack([o[2] for o in outs_p]),
            jnp.stack([o[0] for o in outs_s]), jnp.stack([o[1] for o in outs_s]),
            jnp.stack([o[2] for o in outs_s]))
```

```python
import functools

import jax
import jax.numpy as jnp
from jax import lax
from jax.experimental import pallas as pl
from jax.experimental.pallas import tpu as pltpu

RG_BLK = 256
RG_C = 8.0
CONV4_W = 4
SC_W = 3
EPS = 1e-6

SUBLANES = 8
HALO = SUBLANES
V7X_VMEM_BUDGET = 58 * 1024 * 1024

F32 = jnp.float32
BF16 = jnp.bfloat16


def _params(semantics, vmem_bytes):
    return pltpu.CompilerParams(
        dimension_semantics=semantics,
        vmem_limit_bytes=min(int(vmem_bytes) + (12 << 20), V7X_VMEM_BUDGET),
    )


def _rms(x, g):
    ms = jnp.mean(x * x, axis=-1, keepdims=True)
    return (x * lax.rsqrt(ms + EPS)) * g


def _sigmoid(x):
    return 0.5 * jnp.tanh(0.5 * x) + 0.5


def _norm_kernel(x_ref, g_ref, o_ref):
    o_ref[...] = _rms(x_ref[...], g_ref[...]).astype(o_ref.dtype)


def _norm(x, g, tm=256):
    m, d = x.shape
    tm = min(tm, m)
    return pl.pallas_call(
        _norm_kernel,
        grid=(m // tm,),
        in_specs=[pl.BlockSpec((tm, d), lambda i: (i, 0)),
                  pl.BlockSpec((1, d), lambda i: (0, 0))],
        out_specs=pl.BlockSpec((tm, d), lambda i: (i, 0)),
        out_shape=jax.ShapeDtypeStruct((m, d), BF16),
        compiler_params=_params(("parallel",), 2 * tm * d * 6),
        name="norm0",
    )(x, g)


def _mm_kernel(x_ref, w_ref, o_ref, *, relu2):
    acc = jnp.dot(x_ref[...], w_ref[...].astype(BF16), preferred_element_type=F32)
    if relu2:
        acc = jnp.square(jnp.maximum(acc, 0.0))
    o_ref[...] = acc.astype(o_ref.dtype)


def _mm(x, w, l, out_dtype, relu2, name, tm=1024, tn=512):
    m, k = x.shape
    n = w.shape[2]
    tm, tn = min(tm, m), min(tn, n)
    vm = (2 * (tm * k * 2 + k * tn * 4 + tm * tn * jnp.dtype(out_dtype).itemsize)
          + k * tn * 2 + tm * tn * 4)
    return pl.pallas_call(
        functools.partial(_mm_kernel, relu2=relu2),
        grid=(m // tm, n // tn),
        in_specs=[pl.BlockSpec((tm, k), lambda i, j: (i, 0)),
                  pl.BlockSpec((None, k, tn), lambda i, j: (l, 0, j))],
        out_specs=pl.BlockSpec((tm, tn), lambda i, j: (i, j)),
        out_shape=jax.ShapeDtypeStruct((m, n), out_dtype),
        compiler_params=_params(("parallel", "arbitrary"), vm),
        name=name,
    )(x, w)


def _merge_kernel(ua_ref, ub_ref, wa_ref, wb_ref, ga_ref, gb_ref, o_ref):
    ya = jnp.dot(ua_ref[...], wa_ref[...].astype(BF16), preferred_element_type=F32)
    yb = jnp.dot(ub_ref[...], wb_ref[...].astype(BF16), preferred_element_type=F32)
    m = _sigmoid(ga_ref[...]) * ya + _sigmoid(gb_ref[...]) * yb
    o_ref[...] = m.astype(o_ref.dtype)


def _merge(ua, ub, wa, wb, l, z, off_ga, off_gb, tm=1024, tn=256):
    m, ka = ua.shape
    kb = ub.shape[1]
    n = wa.shape[2]
    tm, tn = min(tm, m), min(tn, n)
    vm = (2 * (tm * ka * 2 + tm * kb * 2 + ka * tn * 4 + kb * tn * 4
               + 2 * tm * tn * 4 + tm * tn * 2)
          + (ka + kb) * tn * 2 + 2 * tm * tn * 4)
    return pl.pallas_call(
        _merge_kernel,
        grid=(m // tm, n // tn),
        in_specs=[pl.BlockSpec((tm, ka), lambda i, j: (i, 0)),
                  pl.BlockSpec((tm, kb), lambda i, j: (i, 0)),
                  pl.BlockSpec((None, ka, tn), lambda i, j: (l, 0, j)),
                  pl.BlockSpec((None, kb, tn), lambda i, j: (l, 0, j)),
                  pl.BlockSpec((tm, tn), lambda i, j: (i, off_ga // tn + j)),
                  pl.BlockSpec((tm, tn), lambda i, j: (i, off_gb // tn + j))],
        out_specs=pl.BlockSpec((tm, tn), lambda i, j: (i, j)),
        out_shape=jax.ShapeDtypeStruct((m, n), BF16),
        compiler_params=_params(("parallel", "arbitrary"), vm),
        name="merge",
    )(ua, ub, wa, wb, z, z)


def _mm_norm_kernel(a_ref, w_ref, x_ref, gpost_ref, gnext_ref, xo_ref, ho_ref, acc_ref,
                    *, nk, te):
    k = pl.program_id(1)

    @pl.when(k == 0)
    def _():
        acc_ref[...] = jnp.dot(a_ref[...], w_ref[...].astype(BF16),
                               preferred_element_type=F32)

    @pl.when(jnp.logical_and(k > 0, k < nk))
    def _():
        acc_ref[...] += jnp.dot(a_ref[...], w_ref[...].astype(BF16),
                                preferred_element_type=F32)

    @pl.when(k >= nk)
    def _():
        r0 = pl.multiple_of((k - nk) * te, te)
        o = acc_ref[pl.ds(r0, te), :]
        x1 = x_ref[...] + _rms(o, gpost_ref[...])
        xo_ref[...] = x1
        ho_ref[...] = _rms(x1, gnext_ref[...]).astype(ho_ref.dtype)


def _mm_norm(a, w, l, x, gpost, gnext, name, tm=1024, tk=512, te=128):
    m, kdim = a.shape
    d = w.shape[2]
    tm, tk = min(tm, m), min(tk, kdim)
    te = min(te, tm)
    nk = kdim // tk
    ne = tm // te
    vm = (2 * (tm * tk * 2 + tk * d * 4 + te * d * 4 + te * d * 4 + te * d * 2)
          + tm * d * 4 + tk * d * 2)

    def slab(i, k):
        return (i * ne + jnp.maximum(k - nk, 0), 0)

    return pl.pallas_call(
        functools.partial(_mm_norm_kernel, nk=nk, te=te),
        grid=(m // tm, nk + ne),
        in_specs=[pl.BlockSpec((tm, tk), lambda i, k: (i, jnp.minimum(k, nk - 1))),
                  pl.BlockSpec((None, tk, d), lambda i, k: (l, jnp.minimum(k, nk - 1), 0)),
                  pl.BlockSpec((te, d), slab),
                  pl.BlockSpec((1, d), lambda i, k: (0, 0)),
                  pl.BlockSpec((1, d), lambda i, k: (0, 0))],
        out_specs=[pl.BlockSpec((te, d), slab),
                   pl.BlockSpec((te, d), slab)],
        out_shape=[jax.ShapeDtypeStruct((m, d), F32),
                   jax.ShapeDtypeStruct((m, d), BF16)],
        scratch_shapes=[pltpu.VMEM((tm, d), F32)],
        compiler_params=_params(("parallel", "arbitrary"), vm),
        name=name,
    )(a, w, x, gpost, gnext)


def _rglru_coeffs(xc, wr_ref, wi_ref, br, bi, lam, a_ref, b_ref):
    sp = jnp.maximum(-lam, 0.0) + jnp.log1p(jnp.exp(-jnp.abs(lam)))
    xcb = xc.astype(BF16)
    for g in range(xc.shape[1] // RG_BLK):
        sl = slice(g * RG_BLK, (g + 1) * RG_BLK)
        xg = xcb[:, sl]
        r = _sigmoid(jnp.dot(xg, wr_ref[g].astype(BF16), preferred_element_type=F32)
                     + br[:, sl])
        i = _sigmoid(jnp.dot(xg, wi_ref[g].astype(BF16), preferred_element_type=F32)
                     + bi[:, sl])
        log_a = (-RG_C * r) * sp[:, sl]
        a = jnp.exp(log_a)
        mult = jnp.sqrt(-jnp.tanh(log_a) * (a * a + 1.0))
        a_ref[:, sl] = a
        b_ref[:, sl] = mult * (i * xc[:, sl])


def _scan8(a, b):
    row = lax.broadcasted_iota(jnp.int32, a.shape, 0)
    for s in (1, 2, 4):
        keep = row >= s
        b = jnp.where(keep, a * pltpu.roll(b, s, 0) + b, b)
        a = jnp.where(keep, a * pltpu.roll(a, s, 0), a)
    return a, b


def _last_row(h):
    return jnp.broadcast_to(h[SUBLANES - 1:SUBLANES, :], h.shape)


def _mix_a_prompt_kernel(x_ref, g_ref, w4_ref, b4_ref, wr_ref, wi_ref, br_ref, bi_ref,
                         lam_ref, u_ref, hn_ref, cn_ref,
                         xpad_ref, a_ref, b_ref, hs_ref, hcar_ref, *, tt):
    c = pl.program_id(2)
    tc = x_ref.shape[1]

    @pl.when(c == 0)
    def _():
        xpad_ref[0:HALO, :] = jnp.zeros((HALO, tc), F32)
        hcar_ref[...] = jnp.zeros((SUBLANES, tc), F32)

    x = x_ref[...]
    xpad_ref[HALO:HALO + tt, :] = x
    w4 = w4_ref[...]
    xc = xpad_ref[HALO - 3:HALO - 3 + tt, :] * w4[0:1, :]
    xc = xc + xpad_ref[HALO - 2:HALO - 2 + tt, :] * w4[1:2, :]
    xc = xc + xpad_ref[HALO - 1:HALO - 1 + tt, :] * w4[2:3, :]
    xc = xc + x * w4[3:4, :]
    xc = xc + b4_ref[...]
    xpad_ref[0:HALO, :] = x_ref[tt - HALO:tt, :]

    _rglru_coeffs(xc, wr_ref, wi_ref, br_ref[...], bi_ref[...], lam_ref[...], a_ref, b_ref)

    def body(q, hprev):
        r0 = pl.multiple_of(q * SUBLANES, SUBLANES)
        a, b = _scan8(a_ref[pl.ds(r0, SUBLANES), :], b_ref[pl.ds(r0, SUBLANES), :])
        h = a * hprev + b
        hs_ref[pl.ds(r0, SUBLANES), :] = h
        return _last_row(h)

    hlast = lax.fori_loop(0, tt // SUBLANES, body, hcar_ref[...])
    hcar_ref[...] = hlast
    u_ref[...] = (jax.nn.gelu(g_ref[...]) * hs_ref[...]).astype(u_ref.dtype)

    @pl.when(c == pl.num_programs(2) - 1)
    def _():
        hn_ref[0] = hlast[0:1, :]
        cn_ref[0] = x_ref[tt - (CONV4_W - 1):tt, :]


def _mix_a_prompt(z, off_x, off_g, l, w4, b4, wr, wi, br, bi, lam, batch, seq,
                  tt=256, tc=1024):
    d = w4.shape[1]
    tt, tc = min(tt, seq), min(tc, d)
    nc = seq // tt
    nj = d // tc
    vec = pl.BlockSpec((1, tc), lambda j, b, c: (0, j))
    blk = pl.BlockSpec((None, tc // RG_BLK, RG_BLK, RG_BLK), lambda j, b, c: (l, j, 0, 0))
    vm = 2 * (2 * tt * tc * 4 + tt * tc * 2 + 2 * tc * RG_BLK * 4) + 8 * (tt + HALO) * tc * 4
    return pl.pallas_call(
        functools.partial(_mix_a_prompt_kernel, tt=tt),
        grid=(nj, batch, nc),
        in_specs=[pl.BlockSpec((tt, tc), lambda j, b, c: (b * nc + c, off_x // tc + j)),
                  pl.BlockSpec((tt, tc), lambda j, b, c: (b * nc + c, off_g // tc + j)),
                  pl.BlockSpec((CONV4_W, tc), lambda j, b, c: (0, j)),
                  vec, blk, blk, vec, vec, vec],
        out_specs=[pl.BlockSpec((tt, tc), lambda j, b, c: (b * nc + c, j)),
                   pl.BlockSpec((1, 1, tc), lambda j, b, c: (b, 0, j)),
                   pl.BlockSpec((1, CONV4_W - 1, tc), lambda j, b, c: (b, 0, j))],
        out_shape=[jax.ShapeDtypeStruct((batch * seq, d), BF16),
                   jax.ShapeDtypeStruct((batch, 1, d), F32),
                   jax.ShapeDtypeStruct((batch, CONV4_W - 1, d), F32)],
        scratch_shapes=[pltpu.VMEM((tt + HALO, tc), F32),
                        pltpu.VMEM((tt, tc), F32),
                        pltpu.VMEM((tt, tc), F32),
                        pltpu.VMEM((tt, tc), F32),
                        pltpu.VMEM((SUBLANES, tc), F32)],
        compiler_params=_params(("parallel", "arbitrary", "arbitrary"), vm),
        name="mix_a_prompt",
    )(z, z, w4, b4, wr, wi, br, bi, lam)


def _mix_a_sample_kernel(x_ref, g_ref, h0_ref, c4_ref, w4_ref, b4_ref, wr_ref, wi_ref,
                         br_ref, bi_ref, lam_ref, u_ref, hn_ref, cn_ref,
                         pad_ref, xc_ref, a_ref, b_ref, hs_ref, *, nseq, seq):
    w4 = w4_ref[...]
    b4 = b4_ref[...]

    def conv(q, carry):
        r0 = pl.multiple_of(q * seq, seq)
        x = x_ref[pl.ds(r0, seq), :]
        pad_ref[HALO - (CONV4_W - 1):HALO, :] = c4_ref[q]
        pad_ref[HALO:HALO + seq, :] = x
        xc = pad_ref[HALO - 3:HALO - 3 + seq, :] * w4[0:1, :]
        xc = xc + pad_ref[HALO - 2:HALO - 2 + seq, :] * w4[1:2, :]
        xc = xc + pad_ref[HALO - 1:HALO - 1 + seq, :] * w4[2:3, :]
        xc = xc + x * w4[3:4, :]
        xc_ref[pl.ds(r0, seq), :] = xc + b4
        cn_ref[q] = pad_ref[HALO + seq - (CONV4_W - 1):HALO + seq, :]
        return carry

    lax.fori_loop(0, nseq, conv, 0)

    _rglru_coeffs(xc_ref[...], wr_ref, wi_ref, br_ref[...], bi_ref[...], lam_ref[...],
                  a_ref, b_ref)

    def body(q, carry):
        r0 = pl.multiple_of(q * seq, seq)
        a, b = _scan8(a_ref[pl.ds(r0, seq), :], b_ref[pl.ds(r0, seq), :])
        h0 = jnp.broadcast_to(h0_ref[pl.ds(q, 1), :], a.shape)
        h = a * h0 + b
        hs_ref[pl.ds(r0, seq), :] = h
        hn_ref[pl.ds(q, 1), :] = h[seq - 1:seq, :]
        return carry

    lax.fori_loop(0, nseq, body, 0)
    u_ref[...] = (jax.nn.gelu(g_ref[...]) * hs_ref[...]).astype(u_ref.dtype)


def _mix_a_sample(z, off_x, off_g, h0, c4, l, w4, b4, wr, wi, br, bi, lam, nseq, seq,
                  tc=512):
    assert seq == SUBLANES
    d = w4.shape[1]
    tc = min(tc, d)
    m = nseq * seq
    nj = d // tc
    vec = pl.BlockSpec((1, tc), lambda j: (0, j))
    blk = pl.BlockSpec((None, tc // RG_BLK, RG_BLK, RG_BLK), lambda j: (l, j, 0, 0))
    vm = 2 * (2 * m * tc * 4 + m * tc * 2 + nseq * tc * 4 * 2 + 2 * nseq * 8 * tc * 4
              + 2 * tc * RG_BLK * 4) + 8 * m * tc * 4
    return pl.pallas_call(
        functools.partial(_mix_a_sample_kernel, nseq=nseq, seq=seq),
        grid=(nj,),
        in_specs=[pl.BlockSpec((m, tc), lambda j: (0, off_x // tc + j)),
                  pl.BlockSpec((m, tc), lambda j: (0, off_g // tc + j)),
                  pl.BlockSpec((nseq, tc), lambda j: (0, j)),
                  pl.BlockSpec((nseq, CONV4_W - 1, tc), lambda j: (0, 0, j)),
                  pl.BlockSpec((CONV4_W, tc), lambda j: (0, j)),
                  vec, blk, blk, vec, vec, vec],
        out_specs=[pl.BlockSpec((m, tc), lambda j: (0, j)),
                   pl.BlockSpec((nseq, tc), lambda j: (0, j)),
                   pl.BlockSpec((nseq, CONV4_W - 1, tc), lambda j: (0, 0, j))],
        out_shape=[jax.ShapeDtypeStruct((m, d), BF16),
                   jax.ShapeDtypeStruct((nseq, d), F32),
                   jax.ShapeDtypeStruct((nseq, CONV4_W - 1, d), F32)],
        scratch_shapes=[pltpu.VMEM((2 * HALO, tc), F32),
                        pltpu.VMEM((m, tc), F32),
                        pltpu.VMEM((m, tc), F32),
                        pltpu.VMEM((m, tc), F32),
                        pltpu.VMEM((m, tc), F32)],
        compiler_params=_params(("parallel",), vm),
        name="mix_a_sample",
    )(z, z, h0, c4, w4, b4, wr, wi, br, bi, lam)


def _mix_b_prompt_kernel(sb_ref, sc_ref, sx_ref, w_ref, u_ref, sn_ref, upad_ref, *, tt):
    c = pl.program_id(2)
    tc = sb_ref.shape[1]

    @pl.when(c == 0)
    def _():
        upad_ref[0:HALO, :] = jnp.zeros((HALO, tc), F32)

    u = sc_ref[...] * sx_ref[...]
    upad_ref[HALO:HALO + tt, :] = u
    w = w_ref[...]
    uc = upad_ref[HALO - 2:HALO - 2 + tt, :] * w[0:1, :]
    uc = uc + upad_ref[HALO - 1:HALO - 1 + tt, :] * w[1:2, :]
    uc = uc + u * w[2:3, :]
    u_ref[...] = (sb_ref[...] * uc).astype(u_ref.dtype)
    upad_ref[0:HALO, :] = upad_ref[tt:tt + HALO, :]

    @pl.when(c == pl.num_programs(2) - 1)
    def _():
        sn_ref[0] = upad_ref[HALO - (SC_W - 1):HALO, :]


def _mix_b_prompt(z, off_b, off_c, off_x, w, batch, seq, tt=512, tc=1024):
    d = w.shape[1]
    tt, tc = min(tt, seq), min(tc, d)
    nc = seq // tt
    nj = d // tc
    vm = 2 * (3 * tt * tc * 4 + tt * tc * 2) + (tt + HALO) * tc * 4 + 4 * tt * tc * 4

    def col(off):
        return pl.BlockSpec((tt, tc), lambda j, b, c: (b * nc + c, off // tc + j))

    return pl.pallas_call(
        functools.partial(_mix_b_prompt_kernel, tt=tt),
        grid=(nj, batch, nc),
        in_specs=[col(off_b), col(off_c), col(off_x),
                  pl.BlockSpec((SC_W, tc), lambda j, b, c: (0, j))],
        out_specs=[pl.BlockSpec((tt, tc), lambda j, b, c: (b * nc + c, j)),
                   pl.BlockSpec((1, SC_W - 1, tc), lambda j, b, c: (b, 0, j))],
        out_shape=[jax.ShapeDtypeStruct((batch * seq, d), BF16),
                   jax.ShapeDtypeStruct((batch, SC_W - 1, d), F32)],
        scratch_shapes=[pltpu.VMEM((tt + HALO, tc), F32)],
        compiler_params=_params(("parallel", "arbitrary", "arbitrary"), vm),
        name="mix_b_prompt",
    )(z, z, z, w)


def _mix_b_sample_kernel(sb_ref, sc_ref, sx_ref, buf_ref, w_ref, u_ref, sn_ref, pad_ref,
                         *, nseq, seq):
    w = w_ref[...]

    def body(q, carry):
        r0 = pl.multiple_of(q * seq, seq)
        u = sc_ref[pl.ds(r0, seq), :] * sx_ref[pl.ds(r0, seq), :]
        pad_ref[HALO - (SC_W - 1):HALO, :] = buf_ref[q]
        pad_ref[HALO:HALO + seq, :] = u
        uc = pad_ref[HALO - 2:HALO - 2 + seq, :] * w[0:1, :]
        uc = uc + pad_ref[HALO - 1:HALO - 1 + seq, :] * w[1:2, :]
        uc = uc + u * w[2:3, :]
        u_ref[pl.ds(r0, seq), :] = (sb_ref[pl.ds(r0, seq), :] * uc).astype(u_ref.dtype)
        sn_ref[q] = pad_ref[HALO + seq - (SC_W - 1):HALO + seq, :]
        return carry

    lax.fori_loop(0, nseq, body, 0)


def _mix_b_sample(z, off_b, off_c, off_x, buf, w, nseq, seq, tc=1024):
    assert seq == SUBLANES
    d = w.shape[1]
    tc = min(tc, d)
    m = nseq * seq
    nj = d // tc
    vm = 2 * (3 * m * tc * 4 + m * tc * 4 + 2 * nseq * 8 * tc * 4)

    def col(off):
        return pl.BlockSpec((m, tc), lambda j: (0, off // tc + j))

    return pl.pallas_call(
        functools.partial(_mix_b_sample_kernel, nseq=nseq, seq=seq),
        grid=(nj,),
        in_specs=[col(off_b), col(off_c), col(off_x),
                  pl.BlockSpec((nseq, SC_W - 1, tc), lambda j: (0, 0, j)),
                  pl.BlockSpec((SC_W, tc), lambda j: (0, j))],
        out_specs=[pl.BlockSpec((m, tc), lambda j: (0, j)),
                   pl.BlockSpec((nseq, SC_W - 1, tc), lambda j: (0, 0, j))],
        out_shape=[jax.ShapeDtypeStruct((m, d), F32),
                   jax.ShapeDtypeStruct((nseq, SC_W - 1, d), F32)],
        scratch_shapes=[pltpu.VMEM((2 * HALO, tc), F32)],
        compiler_params=_params(("parallel",), vm),
        name="mix_b_sample",
    )(z, z, z, buf, w)


def _layer(x, h, l, lw, gnext, group):
    d_rnn = lw["w_conv4"].shape[1]
    d_conv = lw["w_sc"].shape[1]
    off_x, off_g = 0, d_rnn
    off_sb = 2 * d_rnn
    off_sc = off_sb + d_conv
    off_sx = off_sc + d_conv
    off_ga = off_sx + d_conv
    off_gb = off_ga + x.shape[1]

    z = _mm(h, lw["w_in"], l, F32, False, "mm_in")
    mixw = (l, lw["w_conv4"], lw["b_conv4"], lw["w_rg_r"], lw["w_rg_i"], lw["b_rg_r"],
            lw["b_rg_i"], lw["rg_lambda"])
    if group["kind"] == "prompt":
        ua, hn, cn = _mix_a_prompt(z, off_x, off_g, *mixw, group["batch"], group["seq"])
        ub, sn = _mix_b_prompt(z, off_sb, off_sc, off_sx, lw["w_sc"],
                               group["batch"], group["seq"])
        hn = hn.reshape(group["batch"], d_rnn)
    else:
        ua, hn, cn = _mix_a_sample(z, off_x, off_g, group["h0"], group["c4"], *mixw,
                                   group["batch"], group["seq"])
        ub, sn = _mix_b_sample(z, off_sb, off_sc, off_sx, group["sc"], lw["w_sc"],
                               group["batch"], group["seq"])
        ub = ub.astype(BF16)
    m = _merge(ua, ub, lw["w_out_a"], lw["w_out_b"], l, z, off_ga, off_gb)
    x1, hm = _mm_norm(m, lw["w_o"], l, x, lw["norm_mix_post"], lw["norm_mlp_pre"], "mm_o")
    a = _mm(hm, lw["w_mlp_up"], l, BF16, True, "mm_up")
    x2, hnext = _mm_norm(a, lw["w_mlp_down"], l, x1, lw["norm_mlp_post"], gnext, "mm_down")
    return x2, hnext, hn, cn, sn


def kernel(x_prompt, x_sample, state_rglru_h, state_conv4, state_shortconv, norm_mix_pre,
           norm_mix_post, w_in, w_conv4, b_conv4, w_rg_r, b_rg_r, w_rg_i, b_rg_i, rg_lambda,
           w_out_a, w_sc, w_out_b, w_o, norm_mlp_pre, norm_mlp_post, w_mlp_up, w_mlp_down):
    depth = w_in.shape[0]
    bp, sp, d = x_prompt.shape
    bs, ss, _ = x_sample.shape

    xp = x_prompt.reshape(bp * sp, d)
    xs = x_sample.reshape(bs * ss, d)
    hp = _norm(xp, norm_mix_pre[0].reshape(1, d))
    hs = _norm(xs, norm_mix_pre[0].reshape(1, d))

    outs_p, outs_s = [], []
    for l in range(depth):
        lw = {
            "w_in": w_in,
            "w_conv4": w_conv4[l],
            "b_conv4": b_conv4[l].reshape(1, -1),
            "w_rg_r": w_rg_r,
            "w_rg_i": w_rg_i,
            "b_rg_r": b_rg_r[l].reshape(1, -1),
            "b_rg_i": b_rg_i[l].reshape(1, -1),
            "rg_lambda": rg_lambda[l].reshape(1, -1),
            "w_out_a": w_out_a,
            "w_sc": w_sc[l],
            "w_out_b": w_out_b,
            "w_o": w_o,
            "norm_mix_post": norm_mix_post[l].reshape(1, -1),
            "norm_mlp_pre": norm_mlp_pre[l].reshape(1, -1),
            "norm_mlp_post": norm_mlp_post[l].reshape(1, -1),
            "w_mlp_up": w_mlp_up,
            "w_mlp_down": w_mlp_down,
        }
        gnext = norm_mix_pre[(l + 1) % depth].reshape(1, -1)
        gp = {"kind": "prompt", "batch": bp, "seq": sp}
        gs = {"kind": "sample", "batch": bs, "seq": ss, "h0": state_rglru_h[l],
              "c4": state_conv4[l], "sc": state_shortconv[l]}
        xp, hp, hnp, cnp, snp = _layer(xp, hp, l, lw, gnext, gp)
        xs, hs, hns, cns, sns = _layer(xs, hs, l, lw, gnext, gs)
        outs_p.append((hnp, cnp, snp))
        outs_s.append((hns, cns, sns))

    return (xp.reshape(bp, sp, d), xs.reshape(bs, ss, d),
            jnp.stack([o[0] for o in outs_p]), jnp.stack([o[1] for o in outs_p]),
            jnp.stack([o[2] for o in outs_p]),
            jnp.stack([o[0] for o in outs_s]), jnp.stack([o[1] for o in outs_s]),
            jnp.stack([o[2] for o in outs_s]))
```

```python
import functools

import jax
import jax.numpy as jnp
from jax import lax
from jax.experimental import pallas as pl
from jax.experimental.pallas import tpu as pltpu

RG_BLK = 256
RG_C = 8.0
CONV4_W = 4
SC_W = 3
EPS = 1e-6

SUBLANES = 8
V7X_VMEM_BUDGET = 58 * 1024 * 1024

F32 = jnp.float32
BF16 = jnp.bfloat16


def _params(semantics, vmem_bytes):
    return pltpu.CompilerParams(
        dimension_semantics=semantics,
        vmem_limit_bytes=min(int(vmem_bytes) + (12 << 20), V7X_VMEM_BUDGET),
    )


def _rms(x, g):
    ms = jnp.mean(x * x, axis=-1, keepdims=True)
    return (x * lax.rsqrt(ms + EPS)) * g


def _sigmoid(x):
    return 0.5 * jnp.tanh(0.5 * x) + 0.5


def _norm_kernel(x_ref, g_ref, o_ref):
    o_ref[...] = _rms(x_ref[...], g_ref[...]).astype(o_ref.dtype)


def _norm(x, g, tm=256):
    m, d = x.shape
    tm = min(tm, m)
    return pl.pallas_call(
        _norm_kernel,
        grid=(m // tm,),
        in_specs=[pl.BlockSpec((tm, d), lambda i: (i, 0)),
                  pl.BlockSpec((1, d), lambda i: (0, 0))],
        out_specs=pl.BlockSpec((tm, d), lambda i: (i, 0)),
        out_shape=jax.ShapeDtypeStruct((m, d), BF16),
        compiler_params=_params(("parallel",), 2 * tm * d * 6),
        name="norm0",
    )(x, g)


def _mm_kernel(x_ref, w_ref, o_ref, *, relu2):
    acc = jnp.dot(x_ref[...], w_ref[...].astype(BF16), preferred_element_type=F32)
    if relu2:
        acc = jnp.square(jnp.maximum(acc, 0.0))
    o_ref[...] = acc.astype(o_ref.dtype)


def _mm(x, w, l, out_dtype, relu2, name, tm=1024, tn=512):
    m, k = x.shape
    n = w.shape[2]
    tm, tn = min(tm, m), min(tn, n)
    vm = (2 * (tm * k * 2 + k * tn * 4 + tm * tn * jnp.dtype(out_dtype).itemsize)
          + k * tn * 2 + tm * tn * 4)
    return pl.pallas_call(
        functools.partial(_mm_kernel, relu2=relu2),
        grid=(m // tm, n // tn),
        in_specs=[pl.BlockSpec((tm, k), lambda i, j: (i, 0)),
                  pl.BlockSpec((None, k, tn), lambda i, j: (l, 0, j))],
        out_specs=pl.BlockSpec((tm, tn), lambda i, j: (i, j)),
        out_shape=jax.ShapeDtypeStruct((m, n), out_dtype),
        compiler_params=_params(("parallel", "arbitrary"), vm),
        name=name,
    )(x, w)


def _merge_kernel(ua_ref, ub_ref, wa_ref, wb_ref, ga_ref, gb_ref, o_ref):
    ya = jnp.dot(ua_ref[...], wa_ref[...].astype(BF16), preferred_element_type=F32)
    yb = jnp.dot(ub_ref[...], wb_ref[...].astype(BF16), preferred_element_type=F32)
    m = _sigmoid(ga_ref[...]) * ya + _sigmoid(gb_ref[...]) * yb
    o_ref[...] = m.astype(o_ref.dtype)


def _merge(ua, ub, wa, wb, l, z, off_ga, off_gb, tm=1024, tn=256):
    m, ka = ua.shape
    kb = ub.shape[1]
    n = wa.shape[2]
    tm, tn = min(tm, m), min(tn, n)
    vm = (2 * (tm * ka * 2 + tm * kb * 2 + ka * tn * 4 + kb * tn * 4
               + 2 * tm * tn * 4 + tm * tn * 2)
          + (ka + kb) * tn * 2 + 2 * tm * tn * 4)
    return pl.pallas_call(
        _merge_kernel,
        grid=(m // tm, n // tn),
        in_specs=[pl.BlockSpec((tm, ka), lambda i, j: (i, 0)),
                  pl.BlockSpec((tm, kb), lambda i, j: (i, 0)),
                  pl.BlockSpec((None, ka, tn), lambda i, j: (l, 0, j)),
                  pl.BlockSpec((None, kb, tn), lambda i, j: (l, 0, j)),
                  pl.BlockSpec((tm, tn), lambda i, j: (i, off_ga // tn + j)),
                  pl.BlockSpec((tm, tn), lambda i, j: (i, off_gb // tn + j))],
        out_specs=pl.BlockSpec((tm, tn), lambda i, j: (i, j)),
        out_shape=jax.ShapeDtypeStruct((m, n), BF16),
        compiler_params=_params(("parallel", "arbitrary"), vm),
        name="merge",
    )(ua, ub, wa, wb, z, z)


def _mm_norm_kernel(a_ref, w_ref, x_ref, gpost_ref, gnext_ref, xo_ref, ho_ref, acc_ref,
                    *, nk, te):
    k = pl.program_id(1)

    @pl.when(k == 0)
    def _():
        acc_ref[...] = jnp.dot(a_ref[...], w_ref[...].astype(BF16),
                               preferred_element_type=F32)

    @pl.when(jnp.logical_and(k > 0, k < nk))
    def _():
        acc_ref[...] += jnp.dot(a_ref[...], w_ref[...].astype(BF16),
                                preferred_element_type=F32)

    @pl.when(k >= nk)
    def _():
        r0 = pl.multiple_of((k - nk) * te, te)
        o = acc_ref[pl.ds(r0, te), :]
        x1 = x_ref[...] + _rms(o, gpost_ref[...])
        xo_ref[...] = x1
        ho_ref[...] = _rms(x1, gnext_ref[...]).astype(ho_ref.dtype)


def _mm_norm(a, w, l, x, gpost, gnext, name, tm=1024, tk=512, te=128):
    m, kdim = a.shape
    d = w.shape[2]
    tm, tk = min(tm, m), min(tk, kdim)
    te = min(te, tm)
    nk = kdim // tk
    ne = tm // te
    vm = (2 * (tm * tk * 2 + tk * d * 4 + te * d * 4 + te * d * 4 + te * d * 2)
          + tm * d * 4 + tk * d * 2)

    def slab(i, k):
        return (i * ne + jnp.maximum(k - nk, 0), 0)

    return pl.pallas_call(
        functools.partial(_mm_norm_kernel, nk=nk, te=te),
        grid=(m // tm, nk + ne),
        in_specs=[pl.BlockSpec((tm, tk), lambda i, k: (i, jnp.minimum(k, nk - 1))),
                  pl.BlockSpec((None, tk, d), lambda i, k: (l, jnp.minimum(k, nk - 1), 0)),
                  pl.BlockSpec((te, d), slab),
                  pl.BlockSpec((1, d), lambda i, k: (0, 0)),
                  pl.BlockSpec((1, d), lambda i, k: (0, 0))],
        out_specs=[pl.BlockSpec((te, d), slab),
                   pl.BlockSpec((te, d), slab)],
        out_shape=[jax.ShapeDtypeStruct((m, d), F32),
                   jax.ShapeDtypeStruct((m, d), BF16)],
        scratch_shapes=[pltpu.VMEM((tm, d), F32)],
        compiler_params=_params(("parallel", "arbitrary"), vm),
        name=name,
    )(a, w, x, gpost, gnext)


def _softplus_neg(lam):
    return jnp.maximum(-lam, 0.0) + jnp.log1p(jnp.exp(-jnp.abs(lam)))


def _rglru_coeffs(xc, wr_ref, wi_ref, br, bi, sp):
    xcb = xc.astype(BF16)
    a_parts, b_parts = [], []
    for g in range(xc.shape[1] // RG_BLK):
        sl = slice(g * RG_BLK, (g + 1) * RG_BLK)
        xg = xcb[:, sl]
        r = _sigmoid(jnp.dot(xg, wr_ref[g].astype(BF16), preferred_element_type=F32)
                     + br[:, sl])
        i = _sigmoid(jnp.dot(xg, wi_ref[g].astype(BF16), preferred_element_type=F32)
                     + bi[:, sl])
        log_a = (-RG_C * r) * sp[:, sl]
        a = jnp.exp(log_a)
        y = -jnp.tanh(log_a) * (a * a + 1.0)
        mult = jnp.where(y > 0.0, y * lax.rsqrt(y), 0.0)
        a_parts.append(a)
        b_parts.append(mult * (i * xc[:, sl]))
    if len(a_parts) == 1:
        return a_parts[0], b_parts[0]
    return jnp.concatenate(a_parts, axis=1), jnp.concatenate(b_parts, axis=1)


def _conv_taps(taps, w, width):
    acc = taps[0] * w[0:1, :]
    for k in range(1, width):
        acc = acc + taps[k] * w[k:k + 1, :]
    return acc


def _scan8(a, b):
    row = lax.broadcasted_iota(jnp.int32, a.shape, 0)
    for s in (1, 2, 4):
        keep = row >= s
        b = jnp.where(keep, a * pltpu.roll(b, s, 0) + b, b)
        a = jnp.where(keep, a * pltpu.roll(a, s, 0), a)
    return a, b


def _prev_segment(grp):
    row = lax.broadcasted_iota(jnp.int32, grp.shape, 0)
    return jnp.where(row == 0, 0.0, pltpu.roll(grp, 1, 0))


def _mix_a_prompt_kernel(x_ref, g_ref, w4_ref, b4_ref, wr_ref, wi_ref, br_ref, bi_ref,
                         lam_ref, u_ref, hn_ref, cn_ref, xh_ref, hh_ref, pp_ref,
                         *, seq, rc):
    tc = x_ref.shape[1]
    halo = (CONV4_W - 1) * SUBLANES
    w4 = w4_ref[...]
    b4 = b4_ref[...]
    br = br_ref[...]
    bi = bi_ref[...]
    sp = _softplus_neg(lam_ref[...])

    for k in range(CONV4_W - 1):
        grp = x_ref[seq - halo + SUBLANES * k:seq - halo + SUBLANES * (k + 1), :]
        xh_ref[SUBLANES * k:SUBLANES * (k + 1), :] = _prev_segment(grp)

    h = jnp.zeros((SUBLANES, tc), F32)
    p = jnp.ones((SUBLANES, tc), F32)
    for c in range(seq // rc):
        r0 = c * rc
        if c == 0:
            xh_ref[halo:halo + rc, :] = x_ref[0:rc, :]
            taps = [xh_ref[SUBLANES * k:SUBLANES * k + rc, :] for k in range(CONV4_W)]
        else:
            taps = [x_ref[r0 - halo + SUBLANES * k:r0 - halo + SUBLANES * k + rc, :]
                    for k in range(CONV4_W)]
        xc = _conv_taps(taps, w4, CONV4_W) + b4
        a, b = _rglru_coeffs(xc, wr_ref, wi_ref, br, bi, sp)
        for q in range(rc // SUBLANES):
            aq = a[SUBLANES * q:SUBLANES * (q + 1), :]
            h = aq * h + b[SUBLANES * q:SUBLANES * (q + 1), :]
            p = aq * p
            hh_ref[r0 + SUBLANES * q:r0 + SUBLANES * (q + 1), :] = h
            pp_ref[r0 + SUBLANES * q:r0 + SUBLANES * (q + 1), :] = p

    _, hc = _scan8(p, h)
    cin = _prev_segment(hc)
    hn_ref[0] = hc[SUBLANES - 1:SUBLANES, :]

    for c in range(seq // rc):
        r0 = c * rc
        cin_t = jnp.tile(cin, (rc // SUBLANES, 1))
        hfull = hh_ref[r0:r0 + rc, :] + pp_ref[r0:r0 + rc, :] * cin_t
        u_ref[r0:r0 + rc, :] = (jax.nn.gelu(g_ref[r0:r0 + rc, :]) * hfull).astype(u_ref.dtype)

    for k in range(CONV4_W - 1):
        r = seq - halo + SUBLANES * k + SUBLANES - 1
        cn_ref[0, k:k + 1, :] = x_ref[r:r + 1, :]


def _mix_a_prompt(z, off_x, off_g, l, w4, b4, wr, wi, br, bi, lam, batch, seq,
                  tc=256, rc=256):
    d = w4.shape[1]
    tc, rc = min(tc, d), min(rc, seq)
    nj = d // tc
    vec = pl.BlockSpec((1, tc), lambda j, b: (0, j))
    blk = pl.BlockSpec((None, tc // RG_BLK, RG_BLK, RG_BLK), lambda j, b: (l, j, 0, 0))
    vm = 2 * (2 * seq * tc * 4 + seq * tc * 2 + 2 * tc * RG_BLK * 4) + 3 * seq * tc * 4
    return pl.pallas_call(
        functools.partial(_mix_a_prompt_kernel, seq=seq, rc=rc),
        grid=(nj, batch),
        in_specs=[pl.BlockSpec((seq, tc), lambda j, b: (b, off_x // tc + j)),
                  pl.BlockSpec((seq, tc), lambda j, b: (b, off_g // tc + j)),
                  pl.BlockSpec((CONV4_W, tc), lambda j, b: (0, j)),
                  vec, blk, blk, vec, vec, vec],
        out_specs=[pl.BlockSpec((seq, tc), lambda j, b: (b, j)),
                   pl.BlockSpec((1, 1, tc), lambda j, b: (b, 0, j)),
                   pl.BlockSpec((1, CONV4_W - 1, tc), lambda j, b: (b, 0, j))],
        out_shape=[jax.ShapeDtypeStruct((batch * seq, d), BF16),
                   jax.ShapeDtypeStruct((batch, 1, d), F32),
                   jax.ShapeDtypeStruct((batch, CONV4_W - 1, d), F32)],
        scratch_shapes=[pltpu.VMEM(((CONV4_W - 1) * SUBLANES + rc, tc), F32),
                        pltpu.VMEM((seq, tc), F32),
                        pltpu.VMEM((seq, tc), F32)],
        compiler_params=_params(("parallel", "arbitrary"), vm),
        name="mix_a_prompt",
    )(z, z, w4, b4, wr, wi, br, bi, lam)


def _mix_a_sample_kernel(x_ref, g_ref, h0_ref, c4_ref, w4_ref, b4_ref, wr_ref, wi_ref,
                         br_ref, bi_ref, lam_ref, u_ref, hn_ref, cn_ref, xh_ref,
                         *, nseq, seq):
    halo = (CONV4_W - 1) * nseq
    m = nseq * seq
    sp = _softplus_neg(lam_ref[...])
    xh_ref[0:halo, :] = c4_ref[...]
    xh_ref[halo:halo + m, :] = x_ref[...]
    taps = [xh_ref[nseq * k:nseq * k + m, :] for k in range(CONV4_W)]
    xc = _conv_taps(taps, w4_ref[...], CONV4_W) + b4_ref[...]
    a, b = _rglru_coeffs(xc, wr_ref, wi_ref, br_ref[...], bi_ref[...], sp)
    h = h0_ref[...]
    for t in range(seq):
        rows = slice(nseq * t, nseq * (t + 1))
        h = a[rows, :] * h + b[rows, :]
        u_ref[rows, :] = (jax.nn.gelu(g_ref[rows, :]) * h).astype(u_ref.dtype)
    hn_ref[...] = h
    cn_ref[...] = xh_ref[m:m + halo, :]


def _mix_a_sample(z, off_x, off_g, h0, c4, l, w4, b4, wr, wi, br, bi, lam, nseq, seq,
                  tc=256):
    d = w4.shape[1]
    tc = min(tc, d)
    m = nseq * seq
    halo = (CONV4_W - 1) * nseq
    nj = d // tc
    vec = pl.BlockSpec((1, tc), lambda j: (0, j))
    blk = pl.BlockSpec((None, tc // RG_BLK, RG_BLK, RG_BLK), lambda j: (l, j, 0, 0))
    vm = 2 * (2 * m * tc * 4 + m * tc * 2 + nseq * tc * 8 + 2 * halo * tc * 4
              + 2 * tc * RG_BLK * 4) + 8 * m * tc * 4
    return pl.pallas_call(
        functools.partial(_mix_a_sample_kernel, nseq=nseq, seq=seq),
        grid=(nj,),
        in_specs=[pl.BlockSpec((m, tc), lambda j: (0, off_x // tc + j)),
                  pl.BlockSpec((m, tc), lambda j: (0, off_g // tc + j)),
                  pl.BlockSpec((nseq, tc), lambda j: (0, j)),
                  pl.BlockSpec((halo, tc), lambda j: (0, j)),
                  pl.BlockSpec((CONV4_W, tc), lambda j: (0, j)),
                  vec, blk, blk, vec, vec, vec],
        out_specs=[pl.BlockSpec((m, tc), lambda j: (0, j)),
                   pl.BlockSpec((nseq, tc), lambda j: (0, j)),
                   pl.BlockSpec((halo, tc), lambda j: (0, j))],
        out_shape=[jax.ShapeDtypeStruct((m, d), BF16),
                   jax.ShapeDtypeStruct((nseq, d), F32),
                   jax.ShapeDtypeStruct((halo, d), F32)],
        scratch_shapes=[pltpu.VMEM((halo + m, tc), F32)],
        compiler_params=_params(("parallel",), vm),
        name="mix_a_sample",
    )(z, z, h0, c4, w4, b4, wr, wi, br, bi, lam)


def _mix_b_prompt_kernel(sb_ref, sc_ref, sx_ref, w_ref, u_ref, sn_ref, uh_ref, *, seq, rc):
    halo = (SC_W - 1) * SUBLANES
    w = w_ref[...]
    for c in range(seq // rc):
        r0 = c * rc
        uh_ref[halo + r0:halo + r0 + rc, :] = sc_ref[r0:r0 + rc, :] * sx_ref[r0:r0 + rc, :]
    for k in range(SC_W - 1):
        grp = uh_ref[seq + SUBLANES * k:seq + SUBLANES * (k + 1), :]
        uh_ref[SUBLANES * k:SUBLANES * (k + 1), :] = _prev_segment(grp)
    for c in range(seq // rc):
        r0 = c * rc
        taps = [uh_ref[r0 + SUBLANES * k:r0 + SUBLANES * k + rc, :] for k in range(SC_W)]
        uc = _conv_taps(taps, w, SC_W)
        u_ref[r0:r0 + rc, :] = (sb_ref[r0:r0 + rc, :] * uc).astype(u_ref.dtype)
    for k in range(SC_W - 1):
        r = seq + SUBLANES * k + SUBLANES - 1
        sn_ref[0, k:k + 1, :] = uh_ref[r:r + 1, :]


def _mix_b_prompt(z, off_b, off_c, off_x, w, batch, seq, tc=512, rc=512):
    d = w.shape[1]
    tc, rc = min(tc, d), min(rc, seq)
    nj = d // tc
    vm = 2 * (3 * seq * tc * 4 + seq * tc * 2) + (seq + 16) * tc * 4 + 4 * rc * tc * 4

    def col(off):
        return pl.BlockSpec((seq, tc), lambda j, b: (b, off // tc + j))

    return pl.pallas_call(
        functools.partial(_mix_b_prompt_kernel, seq=seq, rc=rc),
        grid=(nj, batch),
        in_specs=[col(off_b), col(off_c), col(off_x),
                  pl.BlockSpec((SC_W, tc), lambda j, b: (0, j))],
        out_specs=[pl.BlockSpec((seq, tc), lambda j, b: (b, j)),
                   pl.BlockSpec((1, SC_W - 1, tc), lambda j, b: (b, 0, j))],
        out_shape=[jax.ShapeDtypeStruct((batch * seq, d), BF16),
                   jax.ShapeDtypeStruct((batch, SC_W - 1, d), F32)],
        scratch_shapes=[pltpu.VMEM(((SC_W - 1) * SUBLANES + seq, tc), F32)],
        compiler_params=_params(("parallel", "arbitrary"), vm),
        name="mix_b_prompt",
    )(z, z, z, w)


def _mix_b_sample_kernel(sb_ref, sc_ref, sx_ref, buf_ref, w_ref, u_ref, sn_ref, uh_ref,
                         *, nseq, seq):
    halo = (SC_W - 1) * nseq
    m = nseq * seq
    uh_ref[0:halo, :] = buf_ref[...]
    uh_ref[halo:halo + m, :] = sc_ref[...] * sx_ref[...]
    taps = [uh_ref[nseq * k:nseq * k + m, :] for k in range(SC_W)]
    uc = _conv_taps(taps, w_ref[...], SC_W)
    u_ref[...] = (sb_ref[...] * uc).astype(u_ref.dtype)
    sn_ref[...] = uh_ref[m:m + halo, :]


def _mix_b_sample(z, off_b, off_c, off_x, buf, w, nseq, seq, tc=512):
    d = w.shape[1]
    tc = min(tc, d)
    m = nseq * seq
    halo = (SC_W - 1) * nseq
    nj = d // tc
    vm = 2 * (3 * m * tc * 4 + m * tc * 2 + 2 * halo * tc * 4) + 4 * m * tc * 4

    def col(off):
        return pl.BlockSpec((m, tc), lambda j: (0, off // tc + j))

    return pl.pallas_call(
        functools.partial(_mix_b_sample_kernel, nseq=nseq, seq=seq),
        grid=(nj,),
        in_specs=[col(off_b), col(off_c), col(off_x),
                  pl.BlockSpec((halo, tc), lambda j: (0, j)),
                  pl.BlockSpec((SC_W, tc), lambda j: (0, j))],
        out_specs=[pl.BlockSpec((m, tc), lambda j: (0, j)),
                   pl.BlockSpec((halo, tc), lambda j: (0, j))],
        out_shape=[jax.ShapeDtypeStruct((m, d), BF16),
                   jax.ShapeDtypeStruct((halo, d), F32)],
        scratch_shapes=[pltpu.VMEM((halo + m, tc), F32)],
        compiler_params=_params(("parallel",), vm),
        name="mix_b_sample",
    )(z, z, z, buf, w)


def _time_major(s):
    b, w, c = s.shape
    return jnp.swapaxes(s, 0, 1).reshape(w * b, c)


def _batch_major(s, b):
    return jnp.swapaxes(s.reshape(-1, b, s.shape[1]), 0, 1)


def _layer(x, h, l, lw, gnext, group):
    d_rnn = lw["w_conv4"].shape[1]
    d_conv = lw["w_sc"].shape[1]
    off_x, off_g = 0, d_rnn
    off_sb = 2 * d_rnn
    off_sc = off_sb + d_conv
    off_sx = off_sc + d_conv
    off_ga = off_sx + d_conv
    off_gb = off_ga + x.shape[1]
    nb, seq = group["batch"], group["seq"]

    z = _mm(h, lw["w_in"], l, F32, False, "mm_in")
    mixw = (l, lw["w_conv4"], lw["b_conv4"], lw["w_rg_r"], lw["w_rg_i"], lw["b_rg_r"],
            lw["b_rg_i"], lw["rg_lambda"])
    if group["kind"] == "prompt":
        ua, hn, cn = _mix_a_prompt(z, off_x, off_g, *mixw, nb, seq)
        ub, sn = _mix_b_prompt(z, off_sb, off_sc, off_sx, lw["w_sc"], nb, seq)
        hn = hn.reshape(nb, d_rnn)
    else:
        ua, hn, cn = _mix_a_sample(z, off_x, off_g, group["h0"], _time_major(group["c4"]),
                                   *mixw, nb, seq)
        ub, sn = _mix_b_sample(z, off_sb, off_sc, off_sx, _time_major(group["sc"]),
                               lw["w_sc"], nb, seq)
        cn = _batch_major(cn, nb)
        sn = _batch_major(sn, nb)
    m = _merge(ua, ub, lw["w_out_a"], lw["w_out_b"], l, z, off_ga, off_gb)
    x1, hm = _mm_norm(m, lw["w_o"], l, x, lw["norm_mix_post"], lw["norm_mlp_pre"], "mm_o")
    a = _mm(hm, lw["w_mlp_up"], l, BF16, True, "mm_up")
    x2, hnext = _mm_norm(a, lw["w_mlp_down"], l, x1, lw["norm_mlp_post"], gnext, "mm_down")
    return x2, hnext, hn, cn, sn


def kernel(x_prompt, x_sample, state_rglru_h, state_conv4, state_shortconv, norm_mix_pre,
           norm_mix_post, w_in, w_conv4, b_conv4, w_rg_r, b_rg_r, w_rg_i, b_rg_i, rg_lambda,
           w_out_a, w_sc, w_out_b, w_o, norm_mlp_pre, norm_mlp_post, w_mlp_up, w_mlp_down):
    depth = w_in.shape[0]
    bp, sp, d = x_prompt.shape
    bs, ss, _ = x_sample.shape
    assert sp % (SUBLANES * SUBLANES) == 0 and bs % SUBLANES == 0

    xp = jnp.swapaxes(x_prompt.reshape(bp, SUBLANES, sp // SUBLANES, d), 1, 2)
    xp = xp.reshape(bp * sp, d)
    xs = jnp.swapaxes(x_sample, 0, 1).reshape(ss * bs, d)
    hp = _norm(xp, norm_mix_pre[0].reshape(1, d))
    hs = _norm(xs, norm_mix_pre[0].reshape(1, d))

    outs_p, outs_s = [], []
    for l in range(depth):
        lw = {
            "w_in": w_in,
            "w_conv4": w_conv4[l],
            "b_conv4": b_conv4[l].reshape(1, -1),
            "w_rg_r": w_rg_r,
            "w_rg_i": w_rg_i,
            "b_rg_r": b_rg_r[l].reshape(1, -1),
            "b_rg_i": b_rg_i[l].reshape(1, -1),
            "rg_lambda": rg_lambda[l].reshape(1, -1),
            "w_out_a": w_out_a,
            "w_sc": w_sc[l],
            "w_out_b": w_out_b,
            "w_o": w_o,
            "norm_mix_post": norm_mix_post[l].reshape(1, -1),
            "norm_mlp_pre": norm_mlp_pre[l].reshape(1, -1),
            "norm_mlp_post": norm_mlp_post[l].reshape(1, -1),
            "w_mlp_up": w_mlp_up,
            "w_mlp_down": w_mlp_down,
        }
        gnext = norm_mix_pre[(l + 1) % depth].reshape(1, -1)
        gp = {"kind": "prompt", "batch": bp, "seq": sp}
        gs = {"kind": "sample", "batch": bs, "seq": ss, "h0": state_rglru_h[l],
              "c4": state_conv4[l], "sc": state_shortconv[l]}
        xp, hp, hnp, cnp, snp = _layer(xp, hp, l, lw, gnext, gp)
        xs, hs, hns, cns, sns = _layer(xs, hs, l, lw, gnext, gs)
        outs_p.append((hnp, cnp, snp))
        outs_s.append((hns, cns, sns))

    yp = jnp.swapaxes(xp.reshape(bp, sp // SUBLANES, SUBLANES, d), 1, 2).reshape(bp, sp, d)
    ys = jnp.swapaxes(xs.reshape(ss, bs, d), 0, 1)
    return (yp, ys,
            jnp.stack([o[0] for o in outs_p]), jnp.stack([o[1] for o in outs_p]),
            jnp.stack([o[2] for o in outs_p]),
            jnp.stack([o[0] for o in outs_s]), jnp.stack([o[1] for o in outs_s]),
            jnp.stack([o[2] for o in outs_s]))
```

```python
import functools

import jax
import jax.numpy as jnp
from jax import lax
from jax.experimental import pallas as pl
from jax.experimental.pallas import tpu as pltpu

RG_BLK = 256
RG_C = 8.0
CONV4_W = 4
SC_W = 3
EPS = 1e-6

SUBLANES = 8
V7X_VMEM_BUDGET = 58 * 1024 * 1024

F32 = jnp.float32
BF16 = jnp.bfloat16


def _params(semantics, vmem_bytes):
    return pltpu.CompilerParams(
        dimension_semantics=semantics,
        vmem_limit_bytes=min(int(vmem_bytes) + (12 << 20), V7X_VMEM_BUDGET),
    )


def _rms(x, g):
    ms = jnp.mean(x * x, axis=-1, keepdims=True)
    return (x * lax.rsqrt(ms + EPS)) * g


def _sigmoid(x):
    return 0.5 * jnp.tanh(0.5 * x) + 0.5


def _norm_kernel(x_ref, g_ref, o_ref):
    o_ref[...] = _rms(x_ref[...], g_ref[...]).astype(o_ref.dtype)


def _norm(x, g, tm=256):
    m, d = x.shape
    tm = min(tm, m)
    return pl.pallas_call(
        _norm_kernel,
        grid=(m // tm,),
        in_specs=[pl.BlockSpec((tm, d), lambda i: (i, 0)),
                  pl.BlockSpec((1, d), lambda i: (0, 0))],
        out_specs=pl.BlockSpec((tm, d), lambda i: (i, 0)),
        out_shape=jax.ShapeDtypeStruct((m, d), BF16),
        compiler_params=_params(("parallel",), 2 * tm * d * 6),
        name="norm0",
    )(x, g)


def _mm_kernel(x_ref, w_ref, o_ref, *, relu2):
    acc = jnp.dot(x_ref[...], w_ref[...].astype(BF16), preferred_element_type=F32)
    if relu2:
        acc = jnp.square(jnp.maximum(acc, 0.0))
    o_ref[...] = acc.astype(o_ref.dtype)


def _mm(x, w, l, out_dtype, relu2, name, tm=1024, tn=512):
    m, k = x.shape
    n = w.shape[2]
    tm, tn = min(tm, m), min(tn, n)
    vm = (2 * (tm * k * 2 + k * tn * 4 + tm * tn * jnp.dtype(out_dtype).itemsize)
          + k * tn * 2 + tm * tn * 4)
    return pl.pallas_call(
        functools.partial(_mm_kernel, relu2=relu2),
        grid=(m // tm, n // tn),
        in_specs=[pl.BlockSpec((tm, k), lambda i, j: (i, 0)),
                  pl.BlockSpec((None, k, tn), lambda i, j: (l, 0, j))],
        out_specs=pl.BlockSpec((tm, tn), lambda i, j: (i, j)),
        out_shape=jax.ShapeDtypeStruct((m, n), out_dtype),
        compiler_params=_params(("parallel", "arbitrary"), vm),
        name=name,
    )(x, w)


def _merge_kernel(ua_ref, ub_ref, wa_ref, wb_ref, ga_ref, gb_ref, o_ref):
    ya = jnp.dot(ua_ref[...], wa_ref[...].astype(BF16), preferred_element_type=F32)
    yb = jnp.dot(ub_ref[...], wb_ref[...].astype(BF16), preferred_element_type=F32)
    m = _sigmoid(ga_ref[...]) * ya + _sigmoid(gb_ref[...]) * yb
    o_ref[...] = m.astype(o_ref.dtype)


def _merge(ua, ub, wa, wb, l, z, off_ga, off_gb, tm=1024, tn=256):
    m, ka = ua.shape
    kb = ub.shape[1]
    n = wa.shape[2]
    tm, tn = min(tm, m), min(tn, n)
    vm = (2 * (tm * ka * 2 + tm * kb * 2 + ka * tn * 4 + kb * tn * 4
               + 2 * tm * tn * 4 + tm * tn * 2)
          + (ka + kb) * tn * 2 + 2 * tm * tn * 4)
    return pl.pallas_call(
        _merge_kernel,
        grid=(m // tm, n // tn),
        in_specs=[pl.BlockSpec((tm, ka), lambda i, j: (i, 0)),
                  pl.BlockSpec((tm, kb), lambda i, j: (i, 0)),
                  pl.BlockSpec((None, ka, tn), lambda i, j: (l, 0, j)),
                  pl.BlockSpec((None, kb, tn), lambda i, j: (l, 0, j)),
                  pl.BlockSpec((tm, tn), lambda i, j: (i, off_ga // tn + j)),
                  pl.BlockSpec((tm, tn), lambda i, j: (i, off_gb // tn + j))],
        out_specs=pl.BlockSpec((tm, tn), lambda i, j: (i, j)),
        out_shape=jax.ShapeDtypeStruct((m, n), BF16),
        compiler_params=_params(("parallel", "arbitrary"), vm),
        name="merge",
    )(ua, ub, wa, wb, z, z)


def _mm_norm_kernel(a_ref, w_ref, x_ref, gpost_ref, gnext_ref, xo_ref, ho_ref, acc_ref,
                    *, nk, te):
    k = pl.program_id(1)

    @pl.when(k == 0)
    def _():
        acc_ref[...] = jnp.dot(a_ref[...], w_ref[...].astype(BF16),
                               preferred_element_type=F32)

    @pl.when(jnp.logical_and(k > 0, k < nk))
    def _():
        acc_ref[...] += jnp.dot(a_ref[...], w_ref[...].astype(BF16),
                                preferred_element_type=F32)

    @pl.when(k >= nk)
    def _():
        r0 = pl.multiple_of((k - nk) * te, te)
        o = acc_ref[pl.ds(r0, te), :]
        x1 = x_ref[...] + _rms(o, gpost_ref[...])
        xo_ref[...] = x1
        ho_ref[...] = _rms(x1, gnext_ref[...]).astype(ho_ref.dtype)


def _mm_norm(a, w, l, x, gpost, gnext, name, tm=1024, tk=512, te=128):
    m, kdim = a.shape
    d = w.shape[2]
    tm, tk = min(tm, m), min(tk, kdim)
    te = min(te, tm)
    nk = kdim // tk
    ne = tm // te
    vm = (2 * (tm * tk * 2 + tk * d * 4 + te * d * 4 + te * d * 4 + te * d * 2)
          + tm * d * 4 + tk * d * 2)

    def slab(i, k):
        return (i * ne + jnp.maximum(k - nk, 0), 0)

    return pl.pallas_call(
        functools.partial(_mm_norm_kernel, nk=nk, te=te),
        grid=(m // tm, nk + ne),
        in_specs=[pl.BlockSpec((tm, tk), lambda i, k: (i, jnp.minimum(k, nk - 1))),
                  pl.BlockSpec((None, tk, d), lambda i, k: (l, jnp.minimum(k, nk - 1), 0)),
                  pl.BlockSpec((te, d), slab),
                  pl.BlockSpec((1, d), lambda i, k: (0, 0)),
                  pl.BlockSpec((1, d), lambda i, k: (0, 0))],
        out_specs=[pl.BlockSpec((te, d), slab),
                   pl.BlockSpec((te, d), slab)],
        out_shape=[jax.ShapeDtypeStruct((m, d), F32),
                   jax.ShapeDtypeStruct((m, d), BF16)],
        scratch_shapes=[pltpu.VMEM((tm, d), F32)],
        compiler_params=_params(("parallel", "arbitrary"), vm),
        name=name,
    )(a, w, x, gpost, gnext)


def _mm_norm_pp_kernel(a_ref, w_ref, x_ref, gpost_ref, gnext_ref, xo_ref, ho_ref,
                       acc0_ref, acc1_ref, *, ni, te):
    p = pl.program_id(0)
    k = pl.program_id(1)
    accs = (acc0_ref, acc1_ref)

    def contract(acc_ref):
        acc_ref[...] += jnp.dot(a_ref[...], w_ref[...].astype(BF16),
                                preferred_element_type=F32)

    def drain(acc_ref):
        r0 = pl.multiple_of(k * te, te)
        o = acc_ref[pl.ds(r0, te), :]
        acc_ref[pl.ds(r0, te), :] = jnp.zeros_like(o)
        x1 = x_ref[...] + _rms(o, gpost_ref[...])
        xo_ref[...] = x1
        ho_ref[...] = _rms(x1, gnext_ref[...]).astype(ho_ref.dtype)

    @pl.when(p == 0)
    def _():
        @pl.when(k == 0)
        def _():
            acc0_ref[...] = jnp.zeros(acc0_ref.shape, F32)
            acc1_ref[...] = jnp.zeros(acc1_ref.shape, F32)
        contract(acc0_ref)

    for parity in (0, 1):
        @pl.when(jnp.logical_and(jnp.logical_and(p > 0, p < ni), p % 2 == parity))
        def _():
            contract(accs[parity])
            drain(accs[1 - parity])

    @pl.when(p == ni)
    def _():
        drain(accs[(ni - 1) % 2])


def _mm_norm_pp(a, w, l, x, gpost, gnext, name, tm=1024, tk=512):
    m, kdim = a.shape
    d = w.shape[2]
    tm, tk = min(tm, m // 2), min(tk, kdim)
    ni = m // tm
    nk = kdim // tk
    te = tm // nk
    assert te % 16 == 0
    vm = (2 * (tm * tk * 2 + tk * d * 4 + te * d * 4 + te * d * 4 + te * d * 2)
          + 2 * tm * d * 4 + tk * d * 2)

    def kk(p, k):
        return jnp.where(p < ni, k, nk - 1)

    def slab(p, k):
        return (jnp.where(p > 0, (p - 1) * nk + k, 0), 0)

    return pl.pallas_call(
        functools.partial(_mm_norm_pp_kernel, ni=ni, te=te),
        grid=(ni + 1, nk),
        in_specs=[pl.BlockSpec((tm, tk), lambda p, k: (jnp.minimum(p, ni - 1), kk(p, k))),
                  pl.BlockSpec((None, tk, d), lambda p, k: (l, kk(p, k), 0)),
                  pl.BlockSpec((te, d), slab),
                  pl.BlockSpec((1, d), lambda p, k: (0, 0)),
                  pl.BlockSpec((1, d), lambda p, k: (0, 0))],
        out_specs=[pl.BlockSpec((te, d), slab),
                   pl.BlockSpec((te, d), slab)],
        out_shape=[jax.ShapeDtypeStruct((m, d), F32),
                   jax.ShapeDtypeStruct((m, d), BF16)],
        scratch_shapes=[pltpu.VMEM((tm, d), F32), pltpu.VMEM((tm, d), F32)],
        compiler_params=_params(("arbitrary", "arbitrary"), vm),
        name=name,
    )(a, w, x, gpost, gnext)


def _softplus_neg(lam):
    return jnp.maximum(-lam, 0.0) + jnp.log1p(jnp.exp(-jnp.abs(lam)))


def _rglru_coeffs(xc, wr_ref, wi_ref, br, bi, sp):
    xcb = xc.astype(BF16)
    a_parts, b_parts = [], []
    for g in range(xc.shape[1] // RG_BLK):
        sl = slice(g * RG_BLK, (g + 1) * RG_BLK)
        xg = xcb[:, sl]
        r = _sigmoid(jnp.dot(xg, wr_ref[g].astype(BF16), preferred_element_type=F32)
                     + br[:, sl])
        i = _sigmoid(jnp.dot(xg, wi_ref[g].astype(BF16), preferred_element_type=F32)
                     + bi[:, sl])
        log_a = (-RG_C * r) * sp[:, sl]
        a = jnp.exp(log_a)
        y = -jnp.tanh(log_a) * (a * a + 1.0)
        mult = jnp.where(y > 0.0, y * lax.rsqrt(y), 0.0)
        a_parts.append(a)
        b_parts.append(mult * (i * xc[:, sl]))
    if len(a_parts) == 1:
        return a_parts[0], b_parts[0]
    return jnp.concatenate(a_parts, axis=1), jnp.concatenate(b_parts, axis=1)


def _conv_taps(taps, w, width):
    acc = taps[0] * w[0:1, :]
    for k in range(1, width):
        acc = acc + taps[k] * w[k:k + 1, :]
    return acc


def _scan8(a, b):
    row = lax.broadcasted_iota(jnp.int32, a.shape, 0)
    for s in (1, 2, 4):
        keep = row >= s
        b = jnp.where(keep, a * pltpu.roll(b, s, 0) + b, b)
        a = jnp.where(keep, a * pltpu.roll(a, s, 0), a)
    return a, b


def _prev_segment(grp):
    row = lax.broadcasted_iota(jnp.int32, grp.shape, 0)
    return jnp.where(row == 0, 0.0, pltpu.roll(grp, 1, 0))


def _mix_a_prompt_kernel(x_ref, g_ref, w4_ref, b4_ref, wr_ref, wi_ref, br_ref, bi_ref,
                         lam_ref, u_ref, hn_ref, cn_ref, xh_ref, hh_ref, pp_ref,
                         *, seq, rc):
    tc = x_ref.shape[1]
    halo = (CONV4_W - 1) * SUBLANES
    w4 = w4_ref[...]
    b4 = b4_ref[...]
    br = br_ref[...]
    bi = bi_ref[...]
    sp = _softplus_neg(lam_ref[...])

    for k in range(CONV4_W - 1):
        grp = x_ref[seq - halo + SUBLANES * k:seq - halo + SUBLANES * (k + 1), :]
        xh_ref[SUBLANES * k:SUBLANES * (k + 1), :] = _prev_segment(grp)

    h = jnp.zeros((SUBLANES, tc), F32)
    p = jnp.ones((SUBLANES, tc), F32)
    for c in range(seq // rc):
        r0 = c * rc
        if c == 0:
            xh_ref[halo:halo + rc, :] = x_ref[0:rc, :]
            taps = [xh_ref[SUBLANES * k:SUBLANES * k + rc, :] for k in range(CONV4_W)]
        else:
            taps = [x_ref[r0 - halo + SUBLANES * k:r0 - halo + SUBLANES * k + rc, :]
                    for k in range(CONV4_W)]
        xc = _conv_taps(taps, w4, CONV4_W) + b4
        a, b = _rglru_coeffs(xc, wr_ref, wi_ref, br, bi, sp)
        for q in range(rc // SUBLANES):
            aq = a[SUBLANES * q:SUBLANES * (q + 1), :]
            h = aq * h + b[SUBLANES * q:SUBLANES * (q + 1), :]
            p = aq * p
            hh_ref[r0 + SUBLANES * q:r0 + SUBLANES * (q + 1), :] = h
            pp_ref[r0 + SUBLANES * q:r0 + SUBLANES * (q + 1), :] = p

    _, hc = _scan8(p, h)
    cin = _prev_segment(hc)
    hn_ref[0] = hc[SUBLANES - 1:SUBLANES, :]

    for c in range(seq // rc):
        r0 = c * rc
        cin_t = jnp.tile(cin, (rc // SUBLANES, 1))
        hfull = hh_ref[r0:r0 + rc, :] + pp_ref[r0:r0 + rc, :] * cin_t
        u_ref[r0:r0 + rc, :] = (jax.nn.gelu(g_ref[r0:r0 + rc, :]) * hfull).astype(u_ref.dtype)

    for k in range(CONV4_W - 1):
        r = seq - halo + SUBLANES * k + SUBLANES - 1
        cn_ref[0, k:k + 1, :] = x_ref[r:r + 1, :]


def _mix_a_prompt(z, off_x, off_g, l, w4, b4, wr, wi, br, bi, lam, batch, seq,
                  tc=256, rc=256):
    d = w4.shape[1]
    tc, rc = min(tc, d), min(rc, seq)
    nj = d // tc
    vec = pl.BlockSpec((1, tc), lambda j, b: (0, j))
    blk = pl.BlockSpec((None, tc // RG_BLK, RG_BLK, RG_BLK), lambda j, b: (l, j, 0, 0))
    vm = 2 * (2 * seq * tc * 4 + seq * tc * 2 + 2 * tc * RG_BLK * 4) + 3 * seq * tc * 4
    return pl.pallas_call(
        functools.partial(_mix_a_prompt_kernel, seq=seq, rc=rc),
        grid=(nj, batch),
        in_specs=[pl.BlockSpec((seq, tc), lambda j, b: (b, off_x // tc + j)),
                  pl.BlockSpec((seq, tc), lambda j, b: (b, off_g // tc + j)),
                  pl.BlockSpec((CONV4_W, tc), lambda j, b: (0, j)),
                  vec, blk, blk, vec, vec, vec],
        out_specs=[pl.BlockSpec((seq, tc), lambda j, b: (b, j)),
                   pl.BlockSpec((1, 1, tc), lambda j, b: (b, 0, j)),
                   pl.BlockSpec((1, CONV4_W - 1, tc), lambda j, b: (b, 0, j))],
        out_shape=[jax.ShapeDtypeStruct((batch * seq, d), BF16),
                   jax.ShapeDtypeStruct((batch, 1, d), F32),
                   jax.ShapeDtypeStruct((batch, CONV4_W - 1, d), F32)],
        scratch_shapes=[pltpu.VMEM(((CONV4_W - 1) * SUBLANES + rc, tc), F32),
                        pltpu.VMEM((seq, tc), F32),
                        pltpu.VMEM((seq, tc), F32)],
        compiler_params=_params(("parallel", "arbitrary"), vm),
        name="mix_a_prompt",
    )(z, z, w4, b4, wr, wi, br, bi, lam)


def _mix_a_sample_kernel(x_ref, g_ref, h0_ref, c4_ref, w4_ref, b4_ref, wr_ref, wi_ref,
                         br_ref, bi_ref, lam_ref, u_ref, hn_ref, cn_ref, xh_ref,
                         *, nseq, seq):
    halo = (CONV4_W - 1) * nseq
    m = nseq * seq
    sp = _softplus_neg(lam_ref[...])
    xh_ref[0:halo, :] = c4_ref[...]
    xh_ref[halo:halo + m, :] = x_ref[...]
    taps = [xh_ref[nseq * k:nseq * k + m, :] for k in range(CONV4_W)]
    xc = _conv_taps(taps, w4_ref[...], CONV4_W) + b4_ref[...]
    a, b = _rglru_coeffs(xc, wr_ref, wi_ref, br_ref[...], bi_ref[...], sp)
    h = h0_ref[...]
    for t in range(seq):
        rows = slice(nseq * t, nseq * (t + 1))
        h = a[rows, :] * h + b[rows, :]
        u_ref[rows, :] = (jax.nn.gelu(g_ref[rows, :]) * h).astype(u_ref.dtype)
    hn_ref[...] = h
    cn_ref[...] = xh_ref[m:m + halo, :]


def _mix_a_sample(z, off_x, off_g, h0, c4, l, w4, b4, wr, wi, br, bi, lam, nseq, seq,
                  tc=256):
    d = w4.shape[1]
    tc = min(tc, d)
    m = nseq * seq
    halo = (CONV4_W - 1) * nseq
    nj = d // tc
    vec = pl.BlockSpec((1, tc), lambda j: (0, j))
    blk = pl.BlockSpec((None, tc // RG_BLK, RG_BLK, RG_BLK), lambda j: (l, j, 0, 0))
    vm = 2 * (2 * m * tc * 4 + m * tc * 2 + nseq * tc * 8 + 2 * halo * tc * 4
              + 2 * tc * RG_BLK * 4) + 8 * m * tc * 4
    return pl.pallas_call(
        functools.partial(_mix_a_sample_kernel, nseq=nseq, seq=seq),
        grid=(nj,),
        in_specs=[pl.BlockSpec((m, tc), lambda j: (0, off_x // tc + j)),
                  pl.BlockSpec((m, tc), lambda j: (0, off_g // tc + j)),
                  pl.BlockSpec((nseq, tc), lambda j: (0, j)),
                  pl.BlockSpec((halo, tc), lambda j: (0, j)),
                  pl.BlockSpec((CONV4_W, tc), lambda j: (0, j)),
                  vec, blk, blk, vec, vec, vec],
        out_specs=[pl.BlockSpec((m, tc), lambda j: (0, j)),
                   pl.BlockSpec((nseq, tc), lambda j: (0, j)),
                   pl.BlockSpec((halo, tc), lambda j: (0, j))],
        out_shape=[jax.ShapeDtypeStruct((m, d), BF16),
                   jax.ShapeDtypeStruct((nseq, d), F32),
                   jax.ShapeDtypeStruct((halo, d), F32)],
        scratch_shapes=[pltpu.VMEM((halo + m, tc), F32)],
        compiler_params=_params(("parallel",), vm),
        name="mix_a_sample",
    )(z, z, h0, c4, w4, b4, wr, wi, br, bi, lam)


def _mix_b_prompt_kernel(sb_ref, sc_ref, sx_ref, w_ref, u_ref, sn_ref, uh_ref, *, seq, rc):
    halo = (SC_W - 1) * SUBLANES
    w = w_ref[...]
    for c in range(seq // rc):
        r0 = c * rc
        uh_ref[halo + r0:halo + r0 + rc, :] = sc_ref[r0:r0 + rc, :] * sx_ref[r0:r0 + rc, :]
    for k in range(SC_W - 1):
        grp = uh_ref[seq + SUBLANES * k:seq + SUBLANES * (k + 1), :]
        uh_ref[SUBLANES * k:SUBLANES * (k + 1), :] = _prev_segment(grp)
    for c in range(seq // rc):
        r0 = c * rc
        taps = [uh_ref[r0 + SUBLANES * k:r0 + SUBLANES * k + rc, :] for k in range(SC_W)]
        uc = _conv_taps(taps, w, SC_W)
        u_ref[r0:r0 + rc, :] = (sb_ref[r0:r0 + rc, :] * uc).astype(u_ref.dtype)
    for k in range(SC_W - 1):
        r = seq + SUBLANES * k + SUBLANES - 1
        sn_ref[0, k:k + 1, :] = uh_ref[r:r + 1, :]


def _mix_b_prompt(z, off_b, off_c, off_x, w, batch, seq, tc=512, rc=512):
    d = w.shape[1]
    tc, rc = min(tc, d), min(rc, seq)
    nj = d // tc
    vm = 2 * (3 * seq * tc * 4 + seq * tc * 2) + (seq + 16) * tc * 4 + 4 * rc * tc * 4

    def col(off):
        return pl.BlockSpec((seq, tc), lambda j, b: (b, off // tc + j))

    return pl.pallas_call(
        functools.partial(_mix_b_prompt_kernel, seq=seq, rc=rc),
        grid=(nj, batch),
        in_specs=[col(off_b), col(off_c), col(off_x),
                  pl.BlockSpec((SC_W, tc), lambda j, b: (0, j))],
        out_specs=[pl.BlockSpec((seq, tc), lambda j, b: (b, j)),
                   pl.BlockSpec((1, SC_W - 1, tc), lambda j, b: (b, 0, j))],
        out_shape=[jax.ShapeDtypeStruct((batch * seq, d), BF16),
                   jax.ShapeDtypeStruct((batch, SC_W - 1, d), F32)],
        scratch_shapes=[pltpu.VMEM(((SC_W - 1) * SUBLANES + seq, tc), F32)],
        compiler_params=_params(("parallel", "arbitrary"), vm),
        name="mix_b_prompt",
    )(z, z, z, w)


def _mix_b_sample_kernel(sb_ref, sc_ref, sx_ref, buf_ref, w_ref, u_ref, sn_ref, uh_ref,
                         *, nseq, seq):
    halo = (SC_W - 1) * nseq
    m = nseq * seq
    uh_ref[0:halo, :] = buf_ref[...]
    uh_ref[halo:halo + m, :] = sc_ref[...] * sx_ref[...]
    taps = [uh_ref[nseq * k:nseq * k + m, :] for k in range(SC_W)]
    uc = _conv_taps(taps, w_ref[...], SC_W)
    u_ref[...] = (sb_ref[...] * uc).astype(u_ref.dtype)
    sn_ref[...] = uh_ref[m:m + halo, :]


def _mix_b_sample(z, off_b, off_c, off_x, buf, w, nseq, seq, tc=512):
    d = w.shape[1]
    tc = min(tc, d)
    m = nseq * seq
    halo = (SC_W - 1) * nseq
    nj = d // tc
    vm = 2 * (3 * m * tc * 4 + m * tc * 2 + 2 * halo * tc * 4) + 4 * m * tc * 4

    def col(off):
        return pl.BlockSpec((m, tc), lambda j: (0, off // tc + j))

    return pl.pallas_call(
        functools.partial(_mix_b_sample_kernel, nseq=nseq, seq=seq),
        grid=(nj,),
        in_specs=[col(off_b), col(off_c), col(off_x),
                  pl.BlockSpec((halo, tc), lambda j: (0, j)),
                  pl.BlockSpec((SC_W, tc), lambda j: (0, j))],
        out_specs=[pl.BlockSpec((m, tc), lambda j: (0, j)),
                   pl.BlockSpec((halo, tc), lambda j: (0, j))],
        out_shape=[jax.ShapeDtypeStruct((m, d), BF16),
                   jax.ShapeDtypeStruct((halo, d), F32)],
        scratch_shapes=[pltpu.VMEM((halo + m, tc), F32)],
        compiler_params=_params(("parallel",), vm),
        name="mix_b_sample",
    )(z, z, z, buf, w)


def _time_major(s):
    b, w, c = s.shape
    return jnp.swapaxes(s, 0, 1).reshape(w * b, c)


def _batch_major(s, b):
    return jnp.swapaxes(s.reshape(-1, b, s.shape[1]), 0, 1)


def _layer(x, h, l, lw, gnext, group):
    d_rnn = lw["w_conv4"].shape[1]
    d_conv = lw["w_sc"].shape[1]
    off_x, off_g = 0, d_rnn
    off_sb = 2 * d_rnn
    off_sc = off_sb + d_conv
    off_sx = off_sc + d_conv
    off_ga = off_sx + d_conv
    off_gb = off_ga + x.shape[1]
    nb, seq = group["batch"], group["seq"]

    z = _mm(h, lw["w_in"], l, F32, False, "mm_in")
    mixw = (l, lw["w_conv4"], lw["b_conv4"], lw["w_rg_r"], lw["w_rg_i"], lw["b_rg_r"],
            lw["b_rg_i"], lw["rg_lambda"])
    if group["kind"] == "prompt":
        ua, hn, cn = _mix_a_prompt(z, off_x, off_g, *mixw, nb, seq)
        ub, sn = _mix_b_prompt(z, off_sb, off_sc, off_sx, lw["w_sc"], nb, seq)
        hn = hn.reshape(nb, d_rnn)
    else:
        ua, hn, cn = _mix_a_sample(z, off_x, off_g, group["h0"], _time_major(group["c4"]),
                                   *mixw, nb, seq)
        ub, sn = _mix_b_sample(z, off_sb, off_sc, off_sx, _time_major(group["sc"]),
                               lw["w_sc"], nb, seq)
        cn = _batch_major(cn, nb)
        sn = _batch_major(sn, nb)
    pp = group["kind"] == "prompt"
    mm_norm = _mm_norm_pp if pp else _mm_norm
    m = _merge(ua, ub, lw["w_out_a"], lw["w_out_b"], l, z, off_ga, off_gb)
    x1, hm = mm_norm(m, lw["w_o"], l, x, lw["norm_mix_post"], lw["norm_mlp_pre"], "mm_o",
                     tk=256 if pp else 512)
    a = _mm(hm, lw["w_mlp_up"], l, BF16, True, "mm_up")
    x2, hnext = mm_norm(a, lw["w_mlp_down"], l, x1, lw["norm_mlp_post"], gnext, "mm_down")
    return x2, hnext, hn, cn, sn


def kernel(x_prompt, x_sample, state_rglru_h, state_conv4, state_shortconv, norm_mix_pre,
           norm_mix_post, w_in, w_conv4, b_conv4, w_rg_r, b_rg_r, w_rg_i, b_rg_i, rg_lambda,
           w_out_a, w_sc, w_out_b, w_o, norm_mlp_pre, norm_mlp_post, w_mlp_up, w_mlp_down):
    depth = w_in.shape[0]
    bp, sp, d = x_prompt.shape
    bs, ss, _ = x_sample.shape
    assert sp % (SUBLANES * SUBLANES) == 0 and bs % SUBLANES == 0

    xp = jnp.swapaxes(x_prompt.reshape(bp, SUBLANES, sp // SUBLANES, d), 1, 2)
    xp = xp.reshape(bp * sp, d)
    xs = jnp.swapaxes(x_sample, 0, 1).reshape(ss * bs, d)
    hp = _norm(xp, norm_mix_pre[0].reshape(1, d))
    hs = _norm(xs, norm_mix_pre[0].reshape(1, d))

    outs_p, outs_s = [], []
    for l in range(depth):
        lw = {
            "w_in": w_in,
            "w_conv4": w_conv4[l],
            "b_conv4": b_conv4[l].reshape(1, -1),
            "w_rg_r": w_rg_r,
            "w_rg_i": w_rg_i,
            "b_rg_r": b_rg_r[l].reshape(1, -1),
            "b_rg_i": b_rg_i[l].reshape(1, -1),
            "rg_lambda": rg_lambda[l].reshape(1, -1),
            "w_out_a": w_out_a,
            "w_sc": w_sc[l],
            "w_out_b": w_out_b,
            "w_o": w_o,
            "norm_mix_post": norm_mix_post[l].reshape(1, -1),
            "norm_mlp_pre": norm_mlp_pre[l].reshape(1, -1),
            "norm_mlp_post": norm_mlp_post[l].reshape(1, -1),
            "w_mlp_up": w_mlp_up,
            "w_mlp_down": w_mlp_down,
        }
        gnext = norm_mix_pre[(l + 1) % depth].reshape(1, -1)
        gp = {"kind": "prompt", "batch": bp, "seq": sp}
        gs = {"kind": "sample", "batch": bs, "seq": ss, "h0": state_rglru_h[l],
              "c4": state_conv4[l], "sc": state_shortconv[l]}
        xp, hp, hnp, cnp, snp = _layer(xp, hp, l, lw, gnext, gp)
        xs, hs, hns, cns, sns = _layer(xs, hs, l, lw, gnext, gs)
        outs_p.append((hnp, cnp, snp))
        outs_s.append((hns, cns, sns))

    yp = jnp.swapaxes(xp.reshape(bp, sp // SUBLANES, SUBLANES, d), 1, 2).reshape(bp, sp, d)
    ys = jnp.swapaxes(xs.reshape(ss, bs, d), 0, 1)
    return (yp, ys,
            jnp.stack([o[0] for o in outs_p]), jnp.stack([o[1] for o in outs_p]),
            jnp.stack([o[2] for o in outs_p]),
            jnp.stack([o[0] for o in outs_s]), jnp.stack([o[1] for o in outs_s]),
            jnp.stack([o[2] for o in outs_s]))
```

```python
import functools

import jax
import jax.numpy as jnp
from jax import lax
from jax.experimental import pallas as pl
from jax.experimental.pallas import tpu as pltpu

RG_BLK = 256
RG_C = 8.0
CONV4_W = 4
SC_W = 3
EPS = 1e-6

SUBLANES = 8
V7X_VMEM_BUDGET = 58 * 1024 * 1024

F32 = jnp.float32
BF16 = jnp.bfloat16


def _params(semantics, vmem_bytes):
    return pltpu.CompilerParams(
        dimension_semantics=semantics,
        vmem_limit_bytes=min(int(vmem_bytes) + (12 << 20), V7X_VMEM_BUDGET),
    )


def _rms(x, g):
    ms = jnp.mean(x * x, axis=-1, keepdims=True)
    return (x * lax.rsqrt(ms + EPS)) * g


def _sigmoid(x):
    return 0.5 * jnp.tanh(0.5 * x) + 0.5


def _norm_kernel(x_ref, g_ref, o_ref):
    o_ref[...] = _rms(x_ref[...], g_ref[...]).astype(o_ref.dtype)


def _norm(x, g, tm=256):
    m, d = x.shape
    tm = min(tm, m)
    return pl.pallas_call(
        _norm_kernel,
        grid=(m // tm,),
        in_specs=[pl.BlockSpec((tm, d), lambda i: (i, 0)),
                  pl.BlockSpec((1, d), lambda i: (0, 0))],
        out_specs=pl.BlockSpec((tm, d), lambda i: (i, 0)),
        out_shape=jax.ShapeDtypeStruct((m, d), BF16),
        compiler_params=_params(("parallel",), 2 * tm * d * 6),
        name="norm0",
    )(x, g)


def _mm_kernel(x_ref, w_ref, o_ref, *, relu2):
    acc = jnp.dot(x_ref[...], w_ref[...].astype(BF16), preferred_element_type=F32)
    if relu2:
        acc = jnp.square(jnp.maximum(acc, 0.0))
    o_ref[...] = acc.astype(o_ref.dtype)


def _mm(x, w, l, out_dtype, relu2, name, tm=2048, tn=512):
    m, k = x.shape
    n = w.shape[2]
    tm, tn = min(tm, m), min(tn, n)
    vm = (tm * k * 2 + 2 * (k * tn * 4 + tm * tn * jnp.dtype(out_dtype).itemsize)
          + k * tn * 2 + tm * tn * 4)
    return pl.pallas_call(
        functools.partial(_mm_kernel, relu2=relu2),
        grid=(m // tm, n // tn),
        in_specs=[pl.BlockSpec((tm, k), lambda i, j: (i, 0), pipeline_mode=pl.Buffered(1)),
                  pl.BlockSpec((None, k, tn), lambda i, j: (l, 0, j))],
        out_specs=pl.BlockSpec((tm, tn), lambda i, j: (i, j)),
        out_shape=jax.ShapeDtypeStruct((m, n), out_dtype),
        compiler_params=_params(("parallel", "arbitrary"), vm),
        name=name,
    )(x, w)


def _merge_kernel(ua_ref, ub_ref, wa_ref, wb_ref, ga_ref, gb_ref, o_ref):
    ya = jnp.dot(ua_ref[...], wa_ref[...].astype(BF16), preferred_element_type=F32)
    yb = jnp.dot(ub_ref[...], wb_ref[...].astype(BF16), preferred_element_type=F32)
    m = _sigmoid(ga_ref[...]) * ya + _sigmoid(gb_ref[...]) * yb
    o_ref[...] = m.astype(o_ref.dtype)


def _merge(ua, ub, wa, wb, l, z, off_ga, off_gb, tm=1024, tn=256):
    m, ka = ua.shape
    kb = ub.shape[1]
    n = wa.shape[2]
    tm, tn = min(tm, m), min(tn, n)
    vm = (2 * (tm * ka * 2 + tm * kb * 2 + ka * tn * 4 + kb * tn * 4
               + 2 * tm * tn * 4 + tm * tn * 2)
          + (ka + kb) * tn * 2 + 2 * tm * tn * 4)
    return pl.pallas_call(
        _merge_kernel,
        grid=(m // tm, n // tn),
        in_specs=[pl.BlockSpec((tm, ka), lambda i, j: (i, 0)),
                  pl.BlockSpec((tm, kb), lambda i, j: (i, 0)),
                  pl.BlockSpec((None, ka, tn), lambda i, j: (l, 0, j)),
                  pl.BlockSpec((None, kb, tn), lambda i, j: (l, 0, j)),
                  pl.BlockSpec((tm, tn), lambda i, j: (i, off_ga // tn + j)),
                  pl.BlockSpec((tm, tn), lambda i, j: (i, off_gb // tn + j))],
        out_specs=pl.BlockSpec((tm, tn), lambda i, j: (i, j)),
        out_shape=jax.ShapeDtypeStruct((m, n), BF16),
        compiler_params=_params(("parallel", "arbitrary"), vm),
        name="merge",
    )(ua, ub, wa, wb, z, z)


def _mm_norm_kernel(a_ref, w_ref, x_ref, gpost_ref, gnext_ref, xo_ref, ho_ref, acc_ref,
                    *, nk, te):
    k = pl.program_id(1)

    @pl.when(k == 0)
    def _():
        acc_ref[...] = jnp.dot(a_ref[...], w_ref[...].astype(BF16),
                               preferred_element_type=F32)

    @pl.when(jnp.logical_and(k > 0, k < nk))
    def _():
        acc_ref[...] += jnp.dot(a_ref[...], w_ref[...].astype(BF16),
                                preferred_element_type=F32)

    @pl.when(k >= nk)
    def _():
        r0 = pl.multiple_of((k - nk) * te, te)
        o = acc_ref[pl.ds(r0, te), :]
        x1 = x_ref[...] + _rms(o, gpost_ref[...])
        xo_ref[...] = x1
        ho_ref[...] = _rms(x1, gnext_ref[...]).astype(ho_ref.dtype)


def _mm_norm(a, w, l, x, gpost, gnext, name, tm=1024, tk=512, te=128):
    m, kdim = a.shape
    d = w.shape[2]
    tm, tk = min(tm, m), min(tk, kdim)
    te = min(te, tm)
    nk = kdim // tk
    ne = tm // te
    vm = (2 * (tm * tk * 2 + tk * d * 4 + te * d * 4 + te * d * 4 + te * d * 2)
          + tm * d * 4 + tk * d * 2)

    def slab(i, k):
        return (i * ne + jnp.maximum(k - nk, 0), 0)

    return pl.pallas_call(
        functools.partial(_mm_norm_kernel, nk=nk, te=te),
        grid=(m // tm, nk + ne),
        in_specs=[pl.BlockSpec((tm, tk), lambda i, k: (i, jnp.minimum(k, nk - 1))),
                  pl.BlockSpec((None, tk, d), lambda i, k: (l, jnp.minimum(k, nk - 1), 0)),
                  pl.BlockSpec((te, d), slab),
                  pl.BlockSpec((1, d), lambda i, k: (0, 0)),
                  pl.BlockSpec((1, d), lambda i, k: (0, 0))],
        out_specs=[pl.BlockSpec((te, d), slab),
                   pl.BlockSpec((te, d), slab)],
        out_shape=[jax.ShapeDtypeStruct((m, d), F32),
                   jax.ShapeDtypeStruct((m, d), BF16)],
        scratch_shapes=[pltpu.VMEM((tm, d), F32)],
        compiler_params=_params(("parallel", "arbitrary"), vm),
        name=name,
    )(a, w, x, gpost, gnext)


def _mm_norm_pp_kernel(a_ref, w_ref, x_ref, gpost_ref, gnext_ref, xo_ref, ho_ref,
                       acc0_ref, acc1_ref, *, ni, te):
    p = pl.program_id(0)
    k = pl.program_id(1)
    accs = (acc0_ref, acc1_ref)

    def contract(acc_ref):
        acc_ref[...] += jnp.dot(a_ref[...], w_ref[...].astype(BF16),
                                preferred_element_type=F32)

    def drain(acc_ref):
        r0 = pl.multiple_of(k * te, te)
        o = acc_ref[pl.ds(r0, te), :]
        acc_ref[pl.ds(r0, te), :] = jnp.zeros_like(o)
        x1 = x_ref[...] + _rms(o, gpost_ref[...])
        xo_ref[...] = x1
        ho_ref[...] = _rms(x1, gnext_ref[...]).astype(ho_ref.dtype)

    @pl.when(p == 0)
    def _():
        @pl.when(k == 0)
        def _():
            acc0_ref[...] = jnp.zeros(acc0_ref.shape, F32)
            acc1_ref[...] = jnp.zeros(acc1_ref.shape, F32)
        contract(acc0_ref)

    for parity in (0, 1):
        @pl.when(jnp.logical_and(jnp.logical_and(p > 0, p < ni), p % 2 == parity))
        def _():
            contract(accs[parity])
            drain(accs[1 - parity])

    @pl.when(p == ni)
    def _():
        drain(accs[(ni - 1) % 2])


def _mm_norm_pp(a, w, l, x, gpost, gnext, name, tm=1024, tk=512):
    m, kdim = a.shape
    d = w.shape[2]
    tm, tk = min(tm, m // 2), min(tk, kdim)
    ni = m // tm
    nk = kdim // tk
    te = tm // nk
    assert te % 16 == 0
    vm = (2 * (tm * tk * 2 + tk * d * 4 + te * d * 4 + te * d * 4 + te * d * 2)
          + 2 * tm * d * 4 + tk * d * 2)

    def kk(p, k):
        return jnp.where(p < ni, k, nk - 1)

    def slab(p, k):
        return (jnp.where(p > 0, (p - 1) * nk + k, 0), 0)

    return pl.pallas_call(
        functools.partial(_mm_norm_pp_kernel, ni=ni, te=te),
        grid=(ni + 1, nk),
        in_specs=[pl.BlockSpec((tm, tk), lambda p, k: (jnp.minimum(p, ni - 1), kk(p, k))),
                  pl.BlockSpec((None, tk, d), lambda p, k: (l, kk(p, k), 0)),
                  pl.BlockSpec((te, d), slab),
                  pl.BlockSpec((1, d), lambda p, k: (0, 0)),
                  pl.BlockSpec((1, d), lambda p, k: (0, 0))],
        out_specs=[pl.BlockSpec((te, d), slab),
                   pl.BlockSpec((te, d), slab)],
        out_shape=[jax.ShapeDtypeStruct((m, d), F32),
                   jax.ShapeDtypeStruct((m, d), BF16)],
        scratch_shapes=[pltpu.VMEM((tm, d), F32), pltpu.VMEM((tm, d), F32)],
        compiler_params=_params(("arbitrary", "arbitrary"), vm),
        name=name,
    )(a, w, x, gpost, gnext)


def _softplus_neg(lam):
    return jnp.maximum(-lam, 0.0) + jnp.log1p(jnp.exp(-jnp.abs(lam)))


def _rglru_coeffs(xc, wr_ref, wi_ref, br, bi, sp):
    xcb = xc.astype(BF16)
    a_parts, b_parts = [], []
    for g in range(xc.shape[1] // RG_BLK):
        sl = slice(g * RG_BLK, (g + 1) * RG_BLK)
        xg = xcb[:, sl]
        r = _sigmoid(jnp.dot(xg, wr_ref[g].astype(BF16), preferred_element_type=F32)
                     + br[:, sl])
        i = _sigmoid(jnp.dot(xg, wi_ref[g].astype(BF16), preferred_element_type=F32)
                     + bi[:, sl])
        log_a = (-RG_C * r) * sp[:, sl]
        a = jnp.exp(log_a)
        y = -jnp.tanh(log_a) * (a * a + 1.0)
        mult = jnp.where(y > 0.0, y * lax.rsqrt(y), 0.0)
        a_parts.append(a)
        b_parts.append(mult * (i * xc[:, sl]))
    if len(a_parts) == 1:
        return a_parts[0], b_parts[0]
    return jnp.concatenate(a_parts, axis=1), jnp.concatenate(b_parts, axis=1)


def _conv_taps(taps, w, width):
    acc = taps[0] * w[0:1, :]
    for k in range(1, width):
        acc = acc + taps[k] * w[k:k + 1, :]
    return acc


def _scan8(a, b):
    row = lax.broadcasted_iota(jnp.int32, a.shape, 0)
    for s in (1, 2, 4):
        keep = row >= s
        b = jnp.where(keep, a * pltpu.roll(b, s, 0) + b, b)
        a = jnp.where(keep, a * pltpu.roll(a, s, 0), a)
    return a, b


def _prev_segment(grp):
    row = lax.broadcasted_iota(jnp.int32, grp.shape, 0)
    return jnp.where(row == 0, 0.0, pltpu.roll(grp, 1, 0))


def _mix_a_prompt_kernel(x_ref, g_ref, w4_ref, b4_ref, wr_ref, wi_ref, br_ref, bi_ref,
                         lam_ref, u_ref, hn_ref, cn_ref, xh_ref, hh_ref, pp_ref,
                         *, seq, rc):
    tc = x_ref.shape[1]
    halo = (CONV4_W - 1) * SUBLANES
    w4 = w4_ref[...]
    b4 = b4_ref[...]
    br = br_ref[...]
    bi = bi_ref[...]
    sp = _softplus_neg(lam_ref[...])

    for k in range(CONV4_W - 1):
        grp = x_ref[seq - halo + SUBLANES * k:seq - halo + SUBLANES * (k + 1), :]
        xh_ref[SUBLANES * k:SUBLANES * (k + 1), :] = _prev_segment(grp)

    h = jnp.zeros((SUBLANES, tc), F32)
    p = jnp.ones((SUBLANES, tc), F32)
    for c in range(seq // rc):
        r0 = c * rc
        if c == 0:
            xh_ref[halo:halo + rc, :] = x_ref[0:rc, :]
            taps = [xh_ref[SUBLANES * k:SUBLANES * k + rc, :] for k in range(CONV4_W)]
        else:
            taps = [x_ref[r0 - halo + SUBLANES * k:r0 - halo + SUBLANES * k + rc, :]
                    for k in range(CONV4_W)]
        xc = _conv_taps(taps, w4, CONV4_W) + b4
        a, b = _rglru_coeffs(xc, wr_ref, wi_ref, br, bi, sp)
        for q in range(rc // SUBLANES):
            aq = a[SUBLANES * q:SUBLANES * (q + 1), :]
            h = aq * h + b[SUBLANES * q:SUBLANES * (q + 1), :]
            p = aq * p
            hh_ref[r0 + SUBLANES * q:r0 + SUBLANES * (q + 1), :] = h
            pp_ref[r0 + SUBLANES * q:r0 + SUBLANES * (q + 1), :] = p

    _, hc = _scan8(p, h)
    cin = _prev_segment(hc)
    hn_ref[0] = hc[SUBLANES - 1:SUBLANES, :]

    for c in range(seq // rc):
        r0 = c * rc
        cin_t = jnp.tile(cin, (rc // SUBLANES, 1))
        hfull = hh_ref[r0:r0 + rc, :] + pp_ref[r0:r0 + rc, :] * cin_t
        u_ref[r0:r0 + rc, :] = (jax.nn.gelu(g_ref[r0:r0 + rc, :]) * hfull).astype(u_ref.dtype)

    for k in range(CONV4_W - 1):
        r = seq - halo + SUBLANES * k + SUBLANES - 1
        cn_ref[0, k:k + 1, :] = x_ref[r:r + 1, :]


def _mix_a_prompt(z, off_x, off_g, l, w4, b4, wr, wi, br, bi, lam, batch, seq,
                  tc=256, rc=256):
    d = w4.shape[1]
    tc, rc = min(tc, d), min(rc, seq)
    nj = d // tc
    vec = pl.BlockSpec((1, tc), lambda j, b: (0, j))
    blk = pl.BlockSpec((None, tc // RG_BLK, RG_BLK, RG_BLK), lambda j, b: (l, j, 0, 0))
    vm = 2 * (2 * seq * tc * 4 + seq * tc * 2 + 2 * tc * RG_BLK * 4) + 3 * seq * tc * 4
    return pl.pallas_call(
        functools.partial(_mix_a_prompt_kernel, seq=seq, rc=rc),
        grid=(nj, batch),
        in_specs=[pl.BlockSpec((seq, tc), lambda j, b: (b, off_x // tc + j)),
                  pl.BlockSpec((seq, tc), lambda j, b: (b, off_g // tc + j)),
                  pl.BlockSpec((CONV4_W, tc), lambda j, b: (0, j)),
                  vec, blk, blk, vec, vec, vec],
        out_specs=[pl.BlockSpec((seq, tc), lambda j, b: (b, j)),
                   pl.BlockSpec((1, 1, tc), lambda j, b: (b, 0, j)),
                   pl.BlockSpec((1, CONV4_W - 1, tc), lambda j, b: (b, 0, j))],
        out_shape=[jax.ShapeDtypeStruct((batch * seq, d), BF16),
                   jax.ShapeDtypeStruct((batch, 1, d), F32),
                   jax.ShapeDtypeStruct((batch, CONV4_W - 1, d), F32)],
        scratch_shapes=[pltpu.VMEM(((CONV4_W - 1) * SUBLANES + rc, tc), F32),
                        pltpu.VMEM((seq, tc), F32),
                        pltpu.VMEM((seq, tc), F32)],
        compiler_params=_params(("parallel", "arbitrary"), vm),
        name="mix_a_prompt",
    )(z, z, w4, b4, wr, wi, br, bi, lam)


def _mix_a_sample_kernel(x_ref, g_ref, h0_ref, c4_ref, w4_ref, b4_ref, wr_ref, wi_ref,
                         br_ref, bi_ref, lam_ref, u_ref, hn_ref, cn_ref, xh_ref,
                         *, nseq, seq):
    halo = (CONV4_W - 1) * nseq
    m = nseq * seq
    sp = _softplus_neg(lam_ref[...])
    xh_ref[0:halo, :] = c4_ref[...]
    xh_ref[halo:halo + m, :] = x_ref[...]
    taps = [xh_ref[nseq * k:nseq * k + m, :] for k in range(CONV4_W)]
    xc = _conv_taps(taps, w4_ref[...], CONV4_W) + b4_ref[...]
    a, b = _rglru_coeffs(xc, wr_ref, wi_ref, br_ref[...], bi_ref[...], sp)
    h = h0_ref[...]
    for t in range(seq):
        rows = slice(nseq * t, nseq * (t + 1))
        h = a[rows, :] * h + b[rows, :]
        u_ref[rows, :] = (jax.nn.gelu(g_ref[rows, :]) * h).astype(u_ref.dtype)
    hn_ref[...] = h
    cn_ref[...] = xh_ref[m:m + halo, :]


def _mix_a_sample(z, off_x, off_g, h0, c4, l, w4, b4, wr, wi, br, bi, lam, nseq, seq,
                  tc=256):
    d = w4.shape[1]
    tc = min(tc, d)
    m = nseq * seq
    halo = (CONV4_W - 1) * nseq
    nj = d // tc
    vec = pl.BlockSpec((1, tc), lambda j: (0, j))
    blk = pl.BlockSpec((None, tc // RG_BLK, RG_BLK, RG_BLK), lambda j: (l, j, 0, 0))
    vm = 2 * (2 * m * tc * 4 + m * tc * 2 + nseq * tc * 8 + 2 * halo * tc * 4
              + 2 * tc * RG_BLK * 4) + 8 * m * tc * 4
    return pl.pallas_call(
        functools.partial(_mix_a_sample_kernel, nseq=nseq, seq=seq),
        grid=(nj,),
        in_specs=[pl.BlockSpec((m, tc), lambda j: (0, off_x // tc + j)),
                  pl.BlockSpec((m, tc), lambda j: (0, off_g // tc + j)),
                  pl.BlockSpec((nseq, tc), lambda j: (0, j)),
                  pl.BlockSpec((halo, tc), lambda j: (0, j)),
                  pl.BlockSpec((CONV4_W, tc), lambda j: (0, j)),
                  vec, blk, blk, vec, vec, vec],
        out_specs=[pl.BlockSpec((m, tc), lambda j: (0, j)),
                   pl.BlockSpec((nseq, tc), lambda j: (0, j)),
                   pl.BlockSpec((halo, tc), lambda j: (0, j))],
        out_shape=[jax.ShapeDtypeStruct((m, d), BF16),
                   jax.ShapeDtypeStruct((nseq, d), F32),
                   jax.ShapeDtypeStruct((halo, d), F32)],
        scratch_shapes=[pltpu.VMEM((halo + m, tc), F32)],
        compiler_params=_params(("parallel",), vm),
        name="mix_a_sample",
    )(z, z, h0, c4, w4, b4, wr, wi, br, bi, lam)


def _mix_b_prompt_kernel(sb_ref, sc_ref, sx_ref, w_ref, u_ref, sn_ref, uh_ref, *, seq, rc):
    halo = (SC_W - 1) * SUBLANES
    w = w_ref[...]
    for c in range(seq // rc):
        r0 = c * rc
        uh_ref[halo + r0:halo + r0 + rc, :] = sc_ref[r0:r0 + rc, :] * sx_ref[r0:r0 + rc, :]
    for k in range(SC_W - 1):
        grp = uh_ref[seq + SUBLANES * k:seq + SUBLANES * (k + 1), :]
        uh_ref[SUBLANES * k:SUBLANES * (k + 1), :] = _prev_segment(grp)
    for c in range(seq // rc):
        r0 = c * rc
        taps = [uh_ref[r0 + SUBLANES * k:r0 + SUBLANES * k + rc, :] for k in range(SC_W)]
        uc = _conv_taps(taps, w, SC_W)
        u_ref[r0:r0 + rc, :] = (sb_ref[r0:r0 + rc, :] * uc).astype(u_ref.dtype)
    for k in range(SC_W - 1):
        r = seq + SUBLANES * k + SUBLANES - 1
        sn_ref[0, k:k + 1, :] = uh_ref[r:r + 1, :]


def _mix_b_prompt(z, off_b, off_c, off_x, w, batch, seq, tc=512, rc=512):
    d = w.shape[1]
    tc, rc = min(tc, d), min(rc, seq)
    nj = d // tc
    vm = 2 * (3 * seq * tc * 4 + seq * tc * 2) + (seq + 16) * tc * 4 + 4 * rc * tc * 4

    def col(off):
        return pl.BlockSpec((seq, tc), lambda j, b: (b, off // tc + j))

    return pl.pallas_call(
        functools.partial(_mix_b_prompt_kernel, seq=seq, rc=rc),
        grid=(nj, batch),
        in_specs=[col(off_b), col(off_c), col(off_x),
                  pl.BlockSpec((SC_W, tc), lambda j, b: (0, j))],
        out_specs=[pl.BlockSpec((seq, tc), lambda j, b: (b, j)),
                   pl.BlockSpec((1, SC_W - 1, tc), lambda j, b: (b, 0, j))],
        out_shape=[jax.ShapeDtypeStruct((batch * seq, d), BF16),
                   jax.ShapeDtypeStruct((batch, SC_W - 1, d), F32)],
        scratch_shapes=[pltpu.VMEM(((SC_W - 1) * SUBLANES + seq, tc), F32)],
        compiler_params=_params(("parallel", "arbitrary"), vm),
        name="mix_b_prompt",
    )(z, z, z, w)


def _mix_b_sample_kernel(sb_ref, sc_ref, sx_ref, buf_ref, w_ref, u_ref, sn_ref, uh_ref,
                         *, nseq, seq):
    halo = (SC_W - 1) * nseq
    m = nseq * seq
    uh_ref[0:halo, :] = buf_ref[...]
    uh_ref[halo:halo + m, :] = sc_ref[...] * sx_ref[...]
    taps = [uh_ref[nseq * k:nseq * k + m, :] for k in range(SC_W)]
    uc = _conv_taps(taps, w_ref[...], SC_W)
    u_ref[...] = (sb_ref[...] * uc).astype(u_ref.dtype)
    sn_ref[...] = uh_ref[m:m + halo, :]


def _mix_b_sample(z, off_b, off_c, off_x, buf, w, nseq, seq, tc=512):
    d = w.shape[1]
    tc = min(tc, d)
    m = nseq * seq
    halo = (SC_W - 1) * nseq
    nj = d // tc
    vm = 2 * (3 * m * tc * 4 + m * tc * 2 + 2 * halo * tc * 4) + 4 * m * tc * 4

    def col(off):
        return pl.BlockSpec((m, tc), lambda j: (0, off // tc + j))

    return pl.pallas_call(
        functools.partial(_mix_b_sample_kernel, nseq=nseq, seq=seq),
        grid=(nj,),
        in_specs=[col(off_b), col(off_c), col(off_x),
                  pl.BlockSpec((halo, tc), lambda j: (0, j)),
                  pl.BlockSpec((SC_W, tc), lambda j: (0, j))],
        out_specs=[pl.BlockSpec((m, tc), lambda j: (0, j)),
                   pl.BlockSpec((halo, tc), lambda j: (0, j))],
        out_shape=[jax.ShapeDtypeStruct((m, d), BF16),
                   jax.ShapeDtypeStruct((halo, d), F32)],
        scratch_shapes=[pltpu.VMEM((halo + m, tc), F32)],
        compiler_params=_params(("parallel",), vm),
        name="mix_b_sample",
    )(z, z, z, buf, w)


def _time_major(s):
    b, w, c = s.shape
    return jnp.swapaxes(s, 0, 1).reshape(w * b, c)


def _batch_major(s, b):
    return jnp.swapaxes(s.reshape(-1, b, s.shape[1]), 0, 1)


def _layer(x, h, l, lw, gnext, group):
    d_rnn = lw["w_conv4"].shape[1]
    d_conv = lw["w_sc"].shape[1]
    off_x, off_g = 0, d_rnn
    off_sb = 2 * d_rnn
    off_sc = off_sb + d_conv
    off_sx = off_sc + d_conv
    off_ga = off_sx + d_conv
    off_gb = off_ga + x.shape[1]
    nb, seq = group["batch"], group["seq"]

    z = _mm(h, lw["w_in"], l, F32, False, "mm_in")
    mixw = (l, lw["w_conv4"], lw["b_conv4"], lw["w_rg_r"], lw["w_rg_i"], lw["b_rg_r"],
            lw["b_rg_i"], lw["rg_lambda"])
    if group["kind"] == "prompt":
        ua, hn, cn = _mix_a_prompt(z, off_x, off_g, *mixw, nb, seq)
        ub, sn = _mix_b_prompt(z, off_sb, off_sc, off_sx, lw["w_sc"], nb, seq)
        hn = hn.reshape(nb, d_rnn)
    else:
        ua, hn, cn = _mix_a_sample(z, off_x, off_g, group["h0"], _time_major(group["c4"]),
                                   *mixw, nb, seq)
        ub, sn = _mix_b_sample(z, off_sb, off_sc, off_sx, _time_major(group["sc"]),
                               lw["w_sc"], nb, seq)
        cn = _batch_major(cn, nb)
        sn = _batch_major(sn, nb)
    pp = group["kind"] == "prompt"
    mm_norm = _mm_norm_pp if pp else _mm_norm
    m = _merge(ua, ub, lw["w_out_a"], lw["w_out_b"], l, z, off_ga, off_gb)
    x1, hm = mm_norm(m, lw["w_o"], l, x, lw["norm_mix_post"], lw["norm_mlp_pre"], "mm_o",
                     tk=256 if pp else 512)
    a = _mm(hm, lw["w_mlp_up"], l, BF16, True, "mm_up")
    x2, hnext = mm_norm(a, lw["w_mlp_down"], l, x1, lw["norm_mlp_post"], gnext, "mm_down")
    return x2, hnext, hn, cn, sn


def kernel(x_prompt, x_sample, state_rglru_h, state_conv4, state_shortconv, norm_mix_pre,
           norm_mix_post, w_in, w_conv4, b_conv4, w_rg_r, b_rg_r, w_rg_i, b_rg_i, rg_lambda,
           w_out_a, w_sc, w_out_b, w_o, norm_mlp_pre, norm_mlp_post, w_mlp_up, w_mlp_down):
    depth = w_in.shape[0]
    bp, sp, d = x_prompt.shape
    bs, ss, _ = x_sample.shape
    assert sp % (SUBLANES * SUBLANES) == 0 and bs % SUBLANES == 0

    xp = jnp.swapaxes(x_prompt.reshape(bp, SUBLANES, sp // SUBLANES, d), 1, 2)
    xp = xp.reshape(bp * sp, d)
    xs = jnp.swapaxes(x_sample, 0, 1).reshape(ss * bs, d)
    hp = _norm(xp, norm_mix_pre[0].reshape(1, d))
    hs = _norm(xs, norm_mix_pre[0].reshape(1, d))

    outs_p, outs_s = [], []
    for l in range(depth):
        lw = {
            "w_in": w_in,
            "w_conv4": w_conv4[l],
            "b_conv4": b_conv4[l].reshape(1, -1),
            "w_rg_r": w_rg_r,
            "w_rg_i": w_rg_i,
            "b_rg_r": b_rg_r[l].reshape(1, -1),
            "b_rg_i": b_rg_i[l].reshape(1, -1),
            "rg_lambda": rg_lambda[l].reshape(1, -1),
            "w_out_a": w_out_a,
            "w_sc": w_sc[l],
            "w_out_b": w_out_b,
            "w_o": w_o,
            "norm_mix_post": norm_mix_post[l].reshape(1, -1),
            "norm_mlp_pre": norm_mlp_pre[l].reshape(1, -1),
            "norm_mlp_post": norm_mlp_post[l].reshape(1, -1),
            "w_mlp_up": w_mlp_up,
            "w_mlp_down": w_mlp_down,
        }
        gnext = norm_mix_pre[(l + 1) % depth].reshape(1, -1)
        gp = {"kind": "prompt", "batch": bp, "seq": sp}
        gs = {"kind": "sample", "batch": bs, "seq": ss, "h0": state_rglru_h[l],
              "c4": state_conv4[l], "sc": state_shortconv[l]}
        xp, hp, hnp, cnp, snp = _layer(xp, hp, l, lw, gnext, gp)
        xs, hs, hns, cns, sns = _layer(xs, hs, l, lw, gnext, gs)
        outs_p.append((hnp, cnp, snp))
        outs_s.append((hns, cns, sns))

    yp = jnp.swapaxes(xp.reshape(bp, sp // SUBLANES, SUBLANES, d), 1, 2).reshape(bp, sp, d)
    ys = jnp.swapaxes(xs.reshape(ss, bs, d), 0, 1)
    return (yp, ys,
            jnp.stack([o[0] for o in outs_p]), jnp.stack([o[1] for o in outs_p]),
            jnp.stack([o[2] for o in outs_p]),
            jnp.stack([o[0] for o in outs_s]), jnp.stack([o[1] for o in outs_s]),
            jnp.stack([o[2] for o in outs_s]))
```

```python
import functools

import jax
import jax.numpy as jnp
from jax import lax
from jax.experimental import pallas as pl
from jax.experimental.pallas import tpu as pltpu

RG_BLK = 256
RG_C = 8.0
CONV4_W = 4
SC_W = 3
EPS = 1e-6

SUBLANES = 8
V7X_VMEM_BUDGET = 58 * 1024 * 1024

F32 = jnp.float32
BF16 = jnp.bfloat16


def _params(semantics, vmem_bytes):
    return pltpu.CompilerParams(
        dimension_semantics=semantics,
        vmem_limit_bytes=min(int(vmem_bytes) + (12 << 20), V7X_VMEM_BUDGET),
    )


def _rms(x, g):
    ms = jnp.mean(x * x, axis=-1, keepdims=True)
    return (x * lax.rsqrt(ms + EPS)) * g


def _sigmoid(x):
    return 0.5 * jnp.tanh(0.5 * x) + 0.5


def _norm_kernel(x_ref, g_ref, o_ref):
    o_ref[...] = _rms(x_ref[...], g_ref[...]).astype(o_ref.dtype)


def _norm(x, g, tm=256):
    m, d = x.shape
    tm = min(tm, m)
    return pl.pallas_call(
        _norm_kernel,
        grid=(m // tm,),
        in_specs=[pl.BlockSpec((tm, d), lambda i: (i, 0)),
                  pl.BlockSpec((1, d), lambda i: (0, 0))],
        out_specs=pl.BlockSpec((tm, d), lambda i: (i, 0)),
        out_shape=jax.ShapeDtypeStruct((m, d), BF16),
        compiler_params=_params(("parallel",), 2 * tm * d * 6),
        name="norm0",
    )(x, g)


def _mm_kernel(x_ref, w_ref, o_ref, *, relu2):
    acc = jnp.dot(x_ref[...], w_ref[...].astype(BF16), preferred_element_type=F32)
    if relu2:
        acc = jnp.square(jnp.maximum(acc, 0.0))
    o_ref[...] = acc.astype(o_ref.dtype)


def _mm(x, w, l, out_dtype, relu2, name, tm=2048, tn=512):
    m, k = x.shape
    n = w.shape[2]
    tm, tn = min(tm, m), min(tn, n)
    vm = (tm * k * 2 + 2 * (k * tn * 4 + tm * tn * jnp.dtype(out_dtype).itemsize)
          + k * tn * 2 + tm * tn * 4)
    return pl.pallas_call(
        functools.partial(_mm_kernel, relu2=relu2),
        grid=(m // tm, n // tn),
        in_specs=[pl.BlockSpec((tm, k), lambda i, j: (i, 0), pipeline_mode=pl.Buffered(1)),
                  pl.BlockSpec((None, k, tn), lambda i, j: (l, 0, j))],
        out_specs=pl.BlockSpec((tm, tn), lambda i, j: (i, j)),
        out_shape=jax.ShapeDtypeStruct((m, n), out_dtype),
        compiler_params=_params(("parallel", "arbitrary"), vm),
        name=name,
    )(x, w)


def _merge_kernel(ua_ref, ub_ref, wa_ref, wb_ref, ga_ref, gb_ref, o_ref):
    ya = jnp.dot(ua_ref[...], wa_ref[...].astype(BF16), preferred_element_type=F32)
    yb = jnp.dot(ub_ref[...], wb_ref[...].astype(BF16), preferred_element_type=F32)
    m = _sigmoid(ga_ref[...]) * ya + _sigmoid(gb_ref[...]) * yb
    o_ref[...] = m.astype(o_ref.dtype)


def _merge(ua, ub, wa, wb, l, z, off_ga, off_gb, tm=1024, tn=256):
    m, ka = ua.shape
    kb = ub.shape[1]
    n = wa.shape[2]
    tm, tn = min(tm, m), min(tn, n)
    vm = (2 * (tm * ka * 2 + tm * kb * 2 + ka * tn * 4 + kb * tn * 4
               + 2 * tm * tn * 4 + tm * tn * 2)
          + (ka + kb) * tn * 2 + 2 * tm * tn * 4)
    return pl.pallas_call(
        _merge_kernel,
        grid=(m // tm, n // tn),
        in_specs=[pl.BlockSpec((tm, ka), lambda i, j: (i, 0)),
                  pl.BlockSpec((tm, kb), lambda i, j: (i, 0)),
                  pl.BlockSpec((None, ka, tn), lambda i, j: (l, 0, j)),
                  pl.BlockSpec((None, kb, tn), lambda i, j: (l, 0, j)),
                  pl.BlockSpec((tm, tn), lambda i, j: (i, off_ga // tn + j)),
                  pl.BlockSpec((tm, tn), lambda i, j: (i, off_gb // tn + j))],
        out_specs=pl.BlockSpec((tm, tn), lambda i, j: (i, j)),
        out_shape=jax.ShapeDtypeStruct((m, n), BF16),
        compiler_params=_params(("parallel", "arbitrary"), vm),
        name="merge",
    )(ua, ub, wa, wb, z, z)


def _mm_norm_kernel(a_ref, w_ref, x_ref, gpost_ref, gnext_ref, xo_ref, ho_ref, acc_ref,
                    *, nk, te):
    k = pl.program_id(1)

    @pl.when(k == 0)
    def _():
        acc_ref[...] = jnp.dot(a_ref[...], w_ref[...].astype(BF16),
                               preferred_element_type=F32)

    @pl.when(jnp.logical_and(k > 0, k < nk))
    def _():
        acc_ref[...] += jnp.dot(a_ref[...], w_ref[...].astype(BF16),
                                preferred_element_type=F32)

    @pl.when(k >= nk)
    def _():
        r0 = pl.multiple_of((k - nk) * te, te)
        o = acc_ref[pl.ds(r0, te), :]
        x1 = x_ref[...] + _rms(o, gpost_ref[...])
        xo_ref[...] = x1
        ho_ref[...] = _rms(x1, gnext_ref[...]).astype(ho_ref.dtype)


def _mm_norm(a, w, l, x, gpost, gnext, name, tm=1024, tk=512, te=128):
    m, kdim = a.shape
    d = w.shape[2]
    tm, tk = min(tm, m), min(tk, kdim)
    te = min(te, tm)
    nk = kdim // tk
    ne = tm // te
    vm = (2 * (tm * tk * 2 + tk * d * 4 + te * d * 4 + te * d * 4 + te * d * 2)
          + tm * d * 4 + tk * d * 2)

    def slab(i, k):
        return (i * ne + jnp.maximum(k - nk, 0), 0)

    return pl.pallas_call(
        functools.partial(_mm_norm_kernel, nk=nk, te=te),
        grid=(m // tm, nk + ne),
        in_specs=[pl.BlockSpec((tm, tk), lambda i, k: (i, jnp.minimum(k, nk - 1))),
                  pl.BlockSpec((None, tk, d), lambda i, k: (l, jnp.minimum(k, nk - 1), 0)),
                  pl.BlockSpec((te, d), slab),
                  pl.BlockSpec((1, d), lambda i, k: (0, 0)),
                  pl.BlockSpec((1, d), lambda i, k: (0, 0))],
        out_specs=[pl.BlockSpec((te, d), slab),
                   pl.BlockSpec((te, d), slab)],
        out_shape=[jax.ShapeDtypeStruct((m, d), F32),
                   jax.ShapeDtypeStruct((m, d), BF16)],
        scratch_shapes=[pltpu.VMEM((tm, d), F32)],
        compiler_params=_params(("parallel", "arbitrary"), vm),
        name=name,
    )(a, w, x, gpost, gnext)


def _mm_norm_pp_kernel(a_ref, w_ref, x_ref, gpost_ref, gnext_ref, xo_ref, ho_ref,
                       acc0_ref, acc1_ref, *, ni, te):
    p = pl.program_id(0)
    k = pl.program_id(1)
    accs = (acc0_ref, acc1_ref)

    def contract(acc_ref):
        acc_ref[...] += jnp.dot(a_ref[...], w_ref[...].astype(BF16),
                                preferred_element_type=F32)

    def drain(acc_ref):
        r0 = pl.multiple_of(k * te, te)
        o = acc_ref[pl.ds(r0, te), :]
        acc_ref[pl.ds(r0, te), :] = jnp.zeros_like(o)
        x1 = x_ref[...] + _rms(o, gpost_ref[...])
        xo_ref[...] = x1
        ho_ref[...] = _rms(x1, gnext_ref[...]).astype(ho_ref.dtype)

    @pl.when(p == 0)
    def _():
        @pl.when(k == 0)
        def _():
            acc0_ref[...] = jnp.zeros(acc0_ref.shape, F32)
            acc1_ref[...] = jnp.zeros(acc1_ref.shape, F32)
        contract(acc0_ref)

    for parity in (0, 1):
        @pl.when(jnp.logical_and(jnp.logical_and(p > 0, p < ni), p % 2 == parity))
        def _():
            contract(accs[parity])
            drain(accs[1 - parity])

    @pl.when(p == ni)
    def _():
        drain(accs[(ni - 1) % 2])


def _mm_norm_pp(a, w, l, x, gpost, gnext, name, tm=1024, tk=512):
    m, kdim = a.shape
    d = w.shape[2]
    tm, tk = min(tm, m // 2), min(tk, kdim)
    ni = m // tm
    nk = kdim // tk
    te = tm // nk
    assert te % 16 == 0
    vm = (2 * (tm * tk * 2 + tk * d * 4 + te * d * 4 + te * d * 4 + te * d * 2)
          + 2 * tm * d * 4 + tk * d * 2)

    def kk(p, k):
        return jnp.where(p < ni, k, nk - 1)

    def slab(p, k):
        return (jnp.where(p > 0, (p - 1) * nk + k, 0), 0)

    return pl.pallas_call(
        functools.partial(_mm_norm_pp_kernel, ni=ni, te=te),
        grid=(ni + 1, nk),
        in_specs=[pl.BlockSpec((tm, tk), lambda p, k: (jnp.minimum(p, ni - 1), kk(p, k))),
                  pl.BlockSpec((None, tk, d), lambda p, k: (l, kk(p, k), 0)),
                  pl.BlockSpec((te, d), slab),
                  pl.BlockSpec((1, d), lambda p, k: (0, 0)),
                  pl.BlockSpec((1, d), lambda p, k: (0, 0))],
        out_specs=[pl.BlockSpec((te, d), slab),
                   pl.BlockSpec((te, d), slab)],
        out_shape=[jax.ShapeDtypeStruct((m, d), F32),
                   jax.ShapeDtypeStruct((m, d), BF16)],
        scratch_shapes=[pltpu.VMEM((tm, d), F32), pltpu.VMEM((tm, d), F32)],
        compiler_params=_params(("arbitrary", "arbitrary"), vm),
        name=name,
    )(a, w, x, gpost, gnext)


def _softplus_neg(lam):
    return jnp.maximum(-lam, 0.0) + jnp.log1p(jnp.exp(-jnp.abs(lam)))


def _rglru_coeffs(xc, wr_ref, wi_ref, br, bi, sp):
    xcb = xc.astype(BF16)
    a_parts, b_parts = [], []
    for g in range(xc.shape[1] // RG_BLK):
        sl = slice(g * RG_BLK, (g + 1) * RG_BLK)
        xg = xcb[:, sl]
        r = _sigmoid(jnp.dot(xg, wr_ref[g].astype(BF16), preferred_element_type=F32)
                     + br[:, sl])
        i = _sigmoid(jnp.dot(xg, wi_ref[g].astype(BF16), preferred_element_type=F32)
                     + bi[:, sl])
        log_a = (-RG_C * r) * sp[:, sl]
        a = jnp.exp(log_a)
        y = -jnp.tanh(log_a) * (a * a + 1.0)
        mult = jnp.where(y > 0.0, y * lax.rsqrt(y), 0.0)
        a_parts.append(a)
        b_parts.append(mult * (i * xc[:, sl]))
    if len(a_parts) == 1:
        return a_parts[0], b_parts[0]
    return jnp.concatenate(a_parts, axis=1), jnp.concatenate(b_parts, axis=1)


def _conv_taps(taps, w, width):
    acc = taps[0] * w[0:1, :]
    for k in range(1, width):
        acc = acc + taps[k] * w[k:k + 1, :]
    return acc


def _scan8(a, b):
    row = lax.broadcasted_iota(jnp.int32, a.shape, 0)
    for s in (1, 2, 4):
        keep = row >= s
        b = jnp.where(keep, a * pltpu.roll(b, s, 0) + b, b)
        a = jnp.where(keep, a * pltpu.roll(a, s, 0), a)
    return a, b


def _prev_segment(grp):
    row = lax.broadcasted_iota(jnp.int32, grp.shape, 0)
    return jnp.where(row == 0, 0.0, pltpu.roll(grp, 1, 0))


def _mix_a_prompt_kernel(x_ref, g_ref, w4_ref, b4_ref, wr_ref, wi_ref, br_ref, bi_ref,
                         lam_ref, u_ref, hn_ref, cn_ref, xh_ref, hh_ref, pp_ref,
                         *, seq, rc):
    tc = x_ref.shape[1]
    halo = (CONV4_W - 1) * SUBLANES
    w4 = w4_ref[...]
    b4 = b4_ref[...]
    br = br_ref[...]
    bi = bi_ref[...]
    sp = _softplus_neg(lam_ref[...])

    for k in range(CONV4_W - 1):
        grp = x_ref[seq - halo + SUBLANES * k:seq - halo + SUBLANES * (k + 1), :]
        xh_ref[SUBLANES * k:SUBLANES * (k + 1), :] = _prev_segment(grp)

    h = jnp.zeros((SUBLANES, tc), F32)
    p = jnp.ones((SUBLANES, tc), F32)
    for c in range(seq // rc):
        r0 = c * rc
        if c == 0:
            xh_ref[halo:halo + rc, :] = x_ref[0:rc, :]
            taps = [xh_ref[SUBLANES * k:SUBLANES * k + rc, :] for k in range(CONV4_W)]
        else:
            taps = [x_ref[r0 - halo + SUBLANES * k:r0 - halo + SUBLANES * k + rc, :]
                    for k in range(CONV4_W)]
        xc = _conv_taps(taps, w4, CONV4_W) + b4
        a, b = _rglru_coeffs(xc, wr_ref, wi_ref, br, bi, sp)
        for q in range(rc // SUBLANES):
            aq = a[SUBLANES * q:SUBLANES * (q + 1), :]
            h = aq * h + b[SUBLANES * q:SUBLANES * (q + 1), :]
            p = aq * p
            hh_ref[r0 + SUBLANES * q:r0 + SUBLANES * (q + 1), :] = h
            pp_ref[r0 + SUBLANES * q:r0 + SUBLANES * (q + 1), :] = p

    _, hc = _scan8(p, h)
    cin = _prev_segment(hc)
    hn_ref[0] = hc[SUBLANES - 1:SUBLANES, :]

    for c in range(seq // rc):
        r0 = c * rc
        cin_t = jnp.tile(cin, (rc // SUBLANES, 1))
        hfull = hh_ref[r0:r0 + rc, :] + pp_ref[r0:r0 + rc, :] * cin_t
        u_ref[r0:r0 + rc, :] = (jax.nn.gelu(g_ref[r0:r0 + rc, :]) * hfull).astype(u_ref.dtype)

    for k in range(CONV4_W - 1):
        r = seq - halo + SUBLANES * k + SUBLANES - 1
        cn_ref[0, k:k + 1, :] = x_ref[r:r + 1, :]


def _mix_a_sample_kernel(x_ref, g_ref, h0_ref, c4_ref, w4_ref, b4_ref, wr_ref, wi_ref,
                         br_ref, bi_ref, lam_ref, u_ref, hn_ref, cn_ref, xh_ref,
                         *, nseq, seq):
    halo = (CONV4_W - 1) * nseq
    m = nseq * seq
    sp = _softplus_neg(lam_ref[...])
    xh_ref[0:halo, :] = c4_ref[...]
    xh_ref[halo:halo + m, :] = x_ref[...]
    taps = [xh_ref[nseq * k:nseq * k + m, :] for k in range(CONV4_W)]
    xc = _conv_taps(taps, w4_ref[...], CONV4_W) + b4_ref[...]
    a, b = _rglru_coeffs(xc, wr_ref, wi_ref, br_ref[...], bi_ref[...], sp)
    h = h0_ref[...]
    for t in range(seq):
        rows = slice(nseq * t, nseq * (t + 1))
        h = a[rows, :] * h + b[rows, :]
        u_ref[rows, :] = (jax.nn.gelu(g_ref[rows, :]) * h).astype(u_ref.dtype)
    hn_ref[...] = h
    cn_ref[...] = xh_ref[m:m + halo, :]


def _mix_b_prompt_kernel(sb_ref, sc_ref, sx_ref, w_ref, u_ref, sn_ref, uh_ref, *, seq, rc):
    halo = (SC_W - 1) * SUBLANES
    w = w_ref[...]
    for c in range(seq // rc):
        r0 = c * rc
        uh_ref[halo + r0:halo + r0 + rc, :] = sc_ref[r0:r0 + rc, :] * sx_ref[r0:r0 + rc, :]
    for k in range(SC_W - 1):
        grp = uh_ref[seq + SUBLANES * k:seq + SUBLANES * (k + 1), :]
        uh_ref[SUBLANES * k:SUBLANES * (k + 1), :] = _prev_segment(grp)
    for c in range(seq // rc):
        r0 = c * rc
        taps = [uh_ref[r0 + SUBLANES * k:r0 + SUBLANES * k + rc, :] for k in range(SC_W)]
        uc = _conv_taps(taps, w, SC_W)
        u_ref[r0:r0 + rc, :] = (sb_ref[r0:r0 + rc, :] * uc).astype(u_ref.dtype)
    for k in range(SC_W - 1):
        r = seq + SUBLANES * k + SUBLANES - 1
        sn_ref[0, k:k + 1, :] = uh_ref[r:r + 1, :]


def _mix_b_sample_kernel(sb_ref, sc_ref, sx_ref, buf_ref, w_ref, u_ref, sn_ref, uh_ref,
                         *, nseq, seq):
    halo = (SC_W - 1) * nseq
    m = nseq * seq
    uh_ref[0:halo, :] = buf_ref[...]
    uh_ref[halo:halo + m, :] = sc_ref[...] * sx_ref[...]
    taps = [uh_ref[nseq * k:nseq * k + m, :] for k in range(SC_W)]
    uc = _conv_taps(taps, w_ref[...], SC_W)
    u_ref[...] = (sb_ref[...] * uc).astype(u_ref.dtype)
    sn_ref[...] = uh_ref[m:m + halo, :]


def _mix_prompt_kernel(x_ref, g_ref, w4_ref, b4_ref, wr_ref, wi_ref, br_ref, bi_ref, lam_ref,
                       sb_ref, sc_ref, sx_ref, wsc_ref,
                       ua_ref, hn_ref, cn_ref, ub_ref, sn_ref,
                       xh_ref, hh_ref, pp_ref, uh_ref, *, seq, rc):
    _mix_b_prompt_kernel(sb_ref, sc_ref, sx_ref, wsc_ref, ub_ref, sn_ref, uh_ref,
                         seq=seq, rc=rc)
    _mix_a_prompt_kernel(x_ref, g_ref, w4_ref, b4_ref, wr_ref, wi_ref, br_ref, bi_ref,
                         lam_ref, ua_ref, hn_ref, cn_ref, xh_ref, hh_ref, pp_ref,
                         seq=seq, rc=rc)


def _mix_prompt(z, offs, l, w4, b4, wr, wi, br, bi, lam, wsc, batch, seq, tc=256, rc=256):
    da, db = w4.shape[1], wsc.shape[1]
    tc, rc = min(tc, da), min(rc, seq)
    nj = da // tc
    tb = db // nj
    vec = pl.BlockSpec((1, tc), lambda j, b: (0, j))
    blk = pl.BlockSpec((None, tc // RG_BLK, RG_BLK, RG_BLK), lambda j, b: (l, j, 0, 0))
    vm = (2 * (2 * seq * tc * 4 + seq * tc * 2 + 2 * tc * RG_BLK * 4 + 3 * seq * tb * 4
               + seq * tb * 2) + 3 * seq * tc * 4 + (seq + 16) * tb * 4)

    def cola(off):
        return pl.BlockSpec((seq, tc), lambda j, b: (b, off // tc + j))

    def colb(off):
        return pl.BlockSpec((seq, tb), lambda j, b: (b, off // tb + j))

    return pl.pallas_call(
        functools.partial(_mix_prompt_kernel, seq=seq, rc=rc),
        grid=(nj, batch),
        in_specs=[cola(offs["x"]), cola(offs["g"]),
                  pl.BlockSpec((CONV4_W, tc), lambda j, b: (0, j)),
                  vec, blk, blk, vec, vec, vec,
                  colb(offs["sb"]), colb(offs["sc"]), colb(offs["sx"]),
                  pl.BlockSpec((SC_W, tb), lambda j, b: (0, j))],
        out_specs=[pl.BlockSpec((seq, tc), lambda j, b: (b, j)),
                   pl.BlockSpec((1, 1, tc), lambda j, b: (b, 0, j)),
                   pl.BlockSpec((1, CONV4_W - 1, tc), lambda j, b: (b, 0, j)),
                   pl.BlockSpec((seq, tb), lambda j, b: (b, j)),
                   pl.BlockSpec((1, SC_W - 1, tb), lambda j, b: (b, 0, j))],
        out_shape=[jax.ShapeDtypeStruct((batch * seq, da), BF16),
                   jax.ShapeDtypeStruct((batch, 1, da), F32),
                   jax.ShapeDtypeStruct((batch, CONV4_W - 1, da), F32),
                   jax.ShapeDtypeStruct((batch * seq, db), BF16),
                   jax.ShapeDtypeStruct((batch, SC_W - 1, db), F32)],
        scratch_shapes=[pltpu.VMEM(((CONV4_W - 1) * SUBLANES + rc, tc), F32),
                        pltpu.VMEM((seq, tc), F32),
                        pltpu.VMEM((seq, tc), F32),
                        pltpu.VMEM(((SC_W - 1) * SUBLANES + seq, tb), F32)],
        compiler_params=_params(("parallel", "arbitrary"), vm),
        name="mix_prompt",
    )(z, z, w4, b4, wr, wi, br, bi, lam, z, z, z, wsc)


def _mix_sample_kernel(x_ref, g_ref, h0_ref, c4_ref, w4_ref, b4_ref, wr_ref, wi_ref, br_ref,
                       bi_ref, lam_ref, sb_ref, sc_ref, sx_ref, buf_ref, wsc_ref,
                       ua_ref, hn_ref, cn_ref, ub_ref, sn_ref, xh_ref, uh_ref, *, nseq, seq):
    _mix_b_sample_kernel(sb_ref, sc_ref, sx_ref, buf_ref, wsc_ref, ub_ref, sn_ref, uh_ref,
                         nseq=nseq, seq=seq)
    _mix_a_sample_kernel(x_ref, g_ref, h0_ref, c4_ref, w4_ref, b4_ref, wr_ref, wi_ref,
                         br_ref, bi_ref, lam_ref, ua_ref, hn_ref, cn_ref, xh_ref,
                         nseq=nseq, seq=seq)


def _mix_sample(z, offs, h0, c4, buf, l, w4, b4, wr, wi, br, bi, lam, wsc, nseq, seq, tc=256):
    da, db = w4.shape[1], wsc.shape[1]
    tc = min(tc, da)
    nj = da // tc
    tb = db // nj
    m = nseq * seq
    ha, hb = (CONV4_W - 1) * nseq, (SC_W - 1) * nseq
    vec = pl.BlockSpec((1, tc), lambda j: (0, j))
    blk = pl.BlockSpec((None, tc // RG_BLK, RG_BLK, RG_BLK), lambda j: (l, j, 0, 0))
    vm = (2 * (2 * m * tc * 4 + m * tc * 2 + nseq * tc * 8 + 2 * ha * tc * 4
               + 2 * tc * RG_BLK * 4 + 3 * m * tb * 4 + m * tb * 2 + 2 * hb * tb * 4)
          + 8 * m * tc * 4 + 4 * m * tb * 4)

    def cola(off):
        return pl.BlockSpec((m, tc), lambda j: (0, off // tc + j))

    def colb(off):
        return pl.BlockSpec((m, tb), lambda j: (0, off // tb + j))

    return pl.pallas_call(
        functools.partial(_mix_sample_kernel, nseq=nseq, seq=seq),
        grid=(nj,),
        in_specs=[cola(offs["x"]), cola(offs["g"]),
                  pl.BlockSpec((nseq, tc), lambda j: (0, j)),
                  pl.BlockSpec((ha, tc), lambda j: (0, j)),
                  pl.BlockSpec((CONV4_W, tc), lambda j: (0, j)),
                  vec, blk, blk, vec, vec, vec,
                  colb(offs["sb"]), colb(offs["sc"]), colb(offs["sx"]),
                  pl.BlockSpec((hb, tb), lambda j: (0, j)),
                  pl.BlockSpec((SC_W, tb), lambda j: (0, j))],
        out_specs=[pl.BlockSpec((m, tc), lambda j: (0, j)),
                   pl.BlockSpec((nseq, tc), lambda j: (0, j)),
                   pl.BlockSpec((ha, tc), lambda j: (0, j)),
                   pl.BlockSpec((m, tb), lambda j: (0, j)),
                   pl.BlockSpec((hb, tb), lambda j: (0, j))],
        out_shape=[jax.ShapeDtypeStruct((m, da), BF16),
                   jax.ShapeDtypeStruct((nseq, da), F32),
                   jax.ShapeDtypeStruct((ha, da), F32),
                   jax.ShapeDtypeStruct((m, db), BF16),
                   jax.ShapeDtypeStruct((hb, db), F32)],
        scratch_shapes=[pltpu.VMEM((ha + m, tc), F32),
                        pltpu.VMEM((hb + m, tb), F32)],
        compiler_params=_params(("parallel",), vm),
        name="mix_sample",
    )(z, z, h0, c4, w4, b4, wr, wi, br, bi, lam, z, z, z, buf, wsc)


def _time_major(s):
    b, w, c = s.shape
    return jnp.swapaxes(s, 0, 1).reshape(w * b, c)


def _batch_major(s, b):
    return jnp.swapaxes(s.reshape(-1, b, s.shape[1]), 0, 1)


def _layer(x, h, l, lw, gnext, group):
    d_rnn = lw["w_conv4"].shape[1]
    d_conv = lw["w_sc"].shape[1]
    off_x, off_g = 0, d_rnn
    off_sb = 2 * d_rnn
    off_sc = off_sb + d_conv
    off_sx = off_sc + d_conv
    off_ga = off_sx + d_conv
    off_gb = off_ga + x.shape[1]
    nb, seq = group["batch"], group["seq"]

    offs = {"x": off_x, "g": off_g, "sb": off_sb, "sc": off_sc, "sx": off_sx}
    z = _mm(h, lw["w_in"], l, F32, False, "mm_in")
    mixw = (l, lw["w_conv4"], lw["b_conv4"], lw["w_rg_r"], lw["w_rg_i"], lw["b_rg_r"],
            lw["b_rg_i"], lw["rg_lambda"], lw["w_sc"])
    if group["kind"] == "prompt":
        ua, hn, cn, ub, sn = _mix_prompt(z, offs, *mixw, nb, seq)
        hn = hn.reshape(nb, d_rnn)
    else:
        ua, hn, cn, ub, sn = _mix_sample(z, offs, group["h0"], _time_major(group["c4"]),
                                         _time_major(group["sc"]), *mixw, nb, seq)
        cn = _batch_major(cn, nb)
        sn = _batch_major(sn, nb)
    pp = group["kind"] == "prompt"
    mm_norm = _mm_norm_pp if pp else _mm_norm
    m = _merge(ua, ub, lw["w_out_a"], lw["w_out_b"], l, z, off_ga, off_gb)
    x1, hm = mm_norm(m, lw["w_o"], l, x, lw["norm_mix_post"], lw["norm_mlp_pre"], "mm_o",
                     tk=256 if pp else 512)
    a = _mm(hm, lw["w_mlp_up"], l, BF16, True, "mm_up")
    x2, hnext = mm_norm(a, lw["w_mlp_down"], l, x1, lw["norm_mlp_post"], gnext, "mm_down")
    return x2, hnext, hn, cn, sn


def kernel(x_prompt, x_sample, state_rglru_h, state_conv4, state_shortconv, norm_mix_pre,
           norm_mix_post, w_in, w_conv4, b_conv4, w_rg_r, b_rg_r, w_rg_i, b_rg_i, rg_lambda,
           w_out_a, w_sc, w_out_b, w_o, norm_mlp_pre, norm_mlp_post, w_mlp_up, w_mlp_down):
    depth = w_in.shape[0]
    bp, sp, d = x_prompt.shape
    bs, ss, _ = x_sample.shape
    assert sp % (SUBLANES * SUBLANES) == 0 and bs % SUBLANES == 0

    xp = jnp.swapaxes(x_prompt.reshape(bp, SUBLANES, sp // SUBLANES, d), 1, 2)
    xp = xp.reshape(bp * sp, d)
    xs = jnp.swapaxes(x_sample, 0, 1).reshape(ss * bs, d)
    hp = _norm(xp, norm_mix_pre[0].reshape(1, d))
    hs = _norm(xs, norm_mix_pre[0].reshape(1, d))

    outs_p, outs_s = [], []
    for l in range(depth):
        lw = {
            "w_in": w_in,
            "w_conv4": w_conv4[l],
            "b_conv4": b_conv4[l].reshape(1, -1),
            "w_rg_r": w_rg_r,
            "w_rg_i": w_rg_i,
            "b_rg_r": b_rg_r[l].reshape(1, -1),
            "b_rg_i": b_rg_i[l].reshape(1, -1),
            "rg_lambda": rg_lambda[l].reshape(1, -1),
            "w_out_a": w_out_a,
            "w_sc": w_sc[l],
            "w_out_b": w_out_b,
            "w_o": w_o,
            "norm_mix_post": norm_mix_post[l].reshape(1, -1),
            "norm_mlp_pre": norm_mlp_pre[l].reshape(1, -1),
            "norm_mlp_post": norm_mlp_post[l].reshape(1, -1),
            "w_mlp_up": w_mlp_up,
            "w_mlp_down": w_mlp_down,
        }
        gnext = norm_mix_pre[(l + 1) % depth].reshape(1, -1)
        gp = {"kind": "prompt", "batch": bp, "seq": sp}
        gs = {"kind": "sample", "batch": bs, "seq": ss, "h0": state_rglru_h[l],
              "c4": state_conv4[l], "sc": state_shortconv[l]}
        xp, hp, hnp, cnp, snp = _layer(xp, hp, l, lw, gnext, gp)
        xs, hs, hns, cns, sns = _layer(xs, hs, l, lw, gnext, gs)
        outs_p.append((hnp, cnp, snp))
        outs_s.append((hns, cns, sns))

    yp = jnp.swapaxes(xp.reshape(bp, sp // SUBLANES, SUBLANES, d), 1, 2).reshape(bp, sp, d)
    ys = jnp.swapaxes(xs.reshape(ss, bs, d), 0, 1)
    return (yp, ys,
            jnp.stack([o[0] for o in outs_p]), jnp.stack([o[1] for o in outs_p]),
            jnp.stack([o[2] for o in outs_p]),
            jnp.stack([o[0] for o in outs_s]), jnp.stack([o[1] for o in outs_s]),
            jnp.stack([o[2] for o in outs_s]))
```

```python
import functools

import jax
import jax.numpy as jnp
from jax import lax
from jax.experimental import pallas as pl
from jax.experimental.pallas import tpu as pltpu

RG_BLK = 256
RG_C = 8.0
CONV4_W = 4
SC_W = 3
EPS = 1e-6

SUBLANES = 8
V7X_VMEM_BUDGET = 58 * 1024 * 1024

F32 = jnp.float32
BF16 = jnp.bfloat16


def _params(semantics, vmem_bytes):
    return pltpu.CompilerParams(
        dimension_semantics=semantics,
        vmem_limit_bytes=min(int(vmem_bytes) + (12 << 20), V7X_VMEM_BUDGET),
    )


def _rms(x, g):
    ms = jnp.mean(x * x, axis=-1, keepdims=True)
    return (x * lax.rsqrt(ms + EPS)) * g


def _sigmoid(x):
    return 0.5 * jnp.tanh(0.5 * x) + 0.5


GELU_C = 0.7978845608028654
GELU_C3 = GELU_C * 0.044715


def _gelu(x):
    hx = 0.5 * x
    return hx * jnp.tanh(x * (GELU_C + GELU_C3 * (x * x))) + hx


def _norm_kernel(x_ref, g_ref, o_ref):
    o_ref[...] = _rms(x_ref[...], g_ref[...]).astype(o_ref.dtype)


def _norm(x, g, tm=256):
    m, d = x.shape
    tm = min(tm, m)
    return pl.pallas_call(
        _norm_kernel,
        grid=(m // tm,),
        in_specs=[pl.BlockSpec((tm, d), lambda i: (i, 0)),
                  pl.BlockSpec((1, d), lambda i: (0, 0))],
        out_specs=pl.BlockSpec((tm, d), lambda i: (i, 0)),
        out_shape=jax.ShapeDtypeStruct((m, d), BF16),
        compiler_params=_params(("parallel",), 2 * tm * d * 6),
        name="norm0",
    )(x, g)


def _mm_kernel(x_ref, w_ref, o_ref, *, relu2):
    acc = jnp.dot(x_ref[...], w_ref[...].astype(BF16), preferred_element_type=F32)
    if relu2:
        acc = jnp.square(jnp.maximum(acc, 0.0))
    o_ref[...] = acc.astype(o_ref.dtype)


def _mm(x, w, l, out_dtype, relu2, name, tm=2048, tn=512, tiled_out=False):
    m, k = x.shape
    n = w.shape[2]
    tm, tn = min(tm, m), min(tn, n)
    if tiled_out:
        out_spec = pl.BlockSpec((None, tm, tn), lambda i, j: (j, i, 0))
        out_shape = jax.ShapeDtypeStruct((n // tn, m, tn), out_dtype)
    else:
        out_spec = pl.BlockSpec((tm, tn), lambda i, j: (i, j))
        out_shape = jax.ShapeDtypeStruct((m, n), out_dtype)
    vm = (tm * k * 2 + 2 * (k * tn * 4 + tm * tn * jnp.dtype(out_dtype).itemsize)
          + k * tn * 2 + tm * tn * 4)
    return pl.pallas_call(
        functools.partial(_mm_kernel, relu2=relu2),
        grid=(m // tm, n // tn),
        in_specs=[pl.BlockSpec((tm, k), lambda i, j: (i, 0), pipeline_mode=pl.Buffered(1)),
                  pl.BlockSpec((None, k, tn), lambda i, j: (l, 0, j))],
        out_specs=out_spec,
        out_shape=out_shape,
        compiler_params=_params(("parallel", "arbitrary"), vm),
        name=name,
    )(x, w)


def _merge_kernel(ua_ref, ub_ref, wa_ref, wb_ref, ga_ref, gb_ref, o_ref):
    ya = jnp.dot(ua_ref[...], wa_ref[...].astype(BF16), preferred_element_type=F32)
    yb = jnp.dot(ub_ref[...], wb_ref[...].astype(BF16), preferred_element_type=F32)
    m = _sigmoid(ga_ref[...]) * ya + _sigmoid(gb_ref[...]) * yb
    o_ref[...] = m.astype(o_ref.dtype)


def _merge(ua, ub, wa, wb, l, z, off_ga, off_gb, tm=1024, tn=256):
    m, ka = ua.shape
    kb = ub.shape[1]
    n = wa.shape[2]
    tm, tn = min(tm, m), min(tn, n)
    vm = (2 * (tm * ka * 2 + tm * kb * 2 + ka * tn * 4 + kb * tn * 4
               + 2 * tm * tn * 4 + tm * tn * 2)
          + (ka + kb) * tn * 2 + 2 * tm * tn * 4)
    return pl.pallas_call(
        _merge_kernel,
        grid=(m // tm, n // tn),
        in_specs=[pl.BlockSpec((tm, ka), lambda i, j: (i, 0)),
                  pl.BlockSpec((tm, kb), lambda i, j: (i, 0)),
                  pl.BlockSpec((None, ka, tn), lambda i, j: (l, 0, j)),
                  pl.BlockSpec((None, kb, tn), lambda i, j: (l, 0, j)),
                  pl.BlockSpec((tm, tn), lambda i, j: (i, off_ga // tn + j)),
                  pl.BlockSpec((tm, tn), lambda i, j: (i, off_gb // tn + j))],
        out_specs=pl.BlockSpec((None, tm, tn), lambda i, j: (j, i, 0)),
        out_shape=jax.ShapeDtypeStruct((n // tn, m, tn), BF16),
        compiler_params=_params(("parallel", "arbitrary"), vm),
        name="merge",
    )(ua, ub, wa, wb, z, z)


def _mm_norm_kernel(a_ref, w_ref, x_ref, gpost_ref, gnext_ref, xo_ref, ho_ref, acc_ref,
                    *, nk, te):
    k = pl.program_id(1)

    @pl.when(k == 0)
    def _():
        acc_ref[...] = jnp.dot(a_ref[...], w_ref[...].astype(BF16),
                               preferred_element_type=F32)

    @pl.when(jnp.logical_and(k > 0, k < nk))
    def _():
        acc_ref[...] += jnp.dot(a_ref[...], w_ref[...].astype(BF16),
                                preferred_element_type=F32)

    @pl.when(k >= nk)
    def _():
        r0 = pl.multiple_of((k - nk) * te, te)
        o = acc_ref[pl.ds(r0, te), :]
        x1 = x_ref[...] + _rms(o, gpost_ref[...])
        xo_ref[...] = x1
        ho_ref[...] = _rms(x1, gnext_ref[...]).astype(ho_ref.dtype)


def _mm_norm(a, w, l, x, gpost, gnext, name, tm=1024, te=128):
    nk, m, tk = a.shape
    d = w.shape[2]
    tm = min(tm, m)
    te = min(te, tm)
    ne = tm // te
    vm = (2 * (tm * tk * 2 + tk * d * 4 + te * d * 4 + te * d * 4 + te * d * 2)
          + tm * d * 4 + tk * d * 2)

    def slab(i, k):
        return (i * ne + jnp.maximum(k - nk, 0), 0)

    return pl.pallas_call(
        functools.partial(_mm_norm_kernel, nk=nk, te=te),
        grid=(m // tm, nk + ne),
        in_specs=[pl.BlockSpec((None, tm, tk), lambda i, k: (jnp.minimum(k, nk - 1), i, 0)),
                  pl.BlockSpec((None, tk, d), lambda i, k: (l, jnp.minimum(k, nk - 1), 0)),
                  pl.BlockSpec((te, d), slab),
                  pl.BlockSpec((1, d), lambda i, k: (0, 0)),
                  pl.BlockSpec((1, d), lambda i, k: (0, 0))],
        out_specs=[pl.BlockSpec((te, d), slab),
                   pl.BlockSpec((te, d), slab)],
        out_shape=[jax.ShapeDtypeStruct((m, d), F32),
                   jax.ShapeDtypeStruct((m, d), BF16)],
        scratch_shapes=[pltpu.VMEM((tm, d), F32)],
        compiler_params=_params(("parallel", "arbitrary"), vm),
        name=name,
    )(a, w, x, gpost, gnext)


def _mm_norm_pp_kernel(a_ref, w_ref, x_ref, gpost_ref, gnext_ref, xo_ref, ho_ref,
                       acc0_ref, acc1_ref, *, ni, te):
    p = pl.program_id(0)
    k = pl.program_id(1)
    accs = (acc0_ref, acc1_ref)

    def contract(acc_ref):
        acc_ref[...] += jnp.dot(a_ref[...], w_ref[...].astype(BF16),
                                preferred_element_type=F32)

    def drain(acc_ref):
        r0 = pl.multiple_of(k * te, te)
        o = acc_ref[pl.ds(r0, te), :]
        acc_ref[pl.ds(r0, te), :] = jnp.zeros_like(o)
        x1 = x_ref[...] + _rms(o, gpost_ref[...])
        xo_ref[...] = x1
        ho_ref[...] = _rms(x1, gnext_ref[...]).astype(ho_ref.dtype)

    @pl.when(p == 0)
    def _():
        @pl.when(k == 0)
        def _():
            acc0_ref[...] = jnp.zeros(acc0_ref.shape, F32)
            acc1_ref[...] = jnp.zeros(acc1_ref.shape, F32)
        contract(acc0_ref)

    for parity in (0, 1):
        @pl.when(jnp.logical_and(jnp.logical_and(p > 0, p < ni), p % 2 == parity))
        def _():
            contract(accs[parity])
            drain(accs[1 - parity])

    @pl.when(p == ni)
    def _():
        drain(accs[(ni - 1) % 2])


def _mm_norm_pp(a, w, l, x, gpost, gnext, name, tm=1024):
    nk, m, tk = a.shape
    d = w.shape[2]
    tm = min(tm, m // 2)
    ni = m // tm
    te = tm // nk
    assert te % 16 == 0
    vm = (2 * (tm * tk * 2 + tk * d * 4 + te * d * 4 + te * d * 4 + te * d * 2)
          + 2 * tm * d * 4 + tk * d * 2)

    def kk(p, k):
        return jnp.where(p < ni, k, nk - 1)

    def slab(p, k):
        return (jnp.where(p > 0, (p - 1) * nk + k, 0), 0)

    return pl.pallas_call(
        functools.partial(_mm_norm_pp_kernel, ni=ni, te=te),
        grid=(ni + 1, nk),
        in_specs=[pl.BlockSpec((None, tm, tk),
                               lambda p, k: (kk(p, k), jnp.minimum(p, ni - 1), 0)),
                  pl.BlockSpec((None, tk, d), lambda p, k: (l, kk(p, k), 0)),
                  pl.BlockSpec((te, d), slab),
                  pl.BlockSpec((1, d), lambda p, k: (0, 0)),
                  pl.BlockSpec((1, d), lambda p, k: (0, 0))],
        out_specs=[pl.BlockSpec((te, d), slab),
                   pl.BlockSpec((te, d), slab)],
        out_shape=[jax.ShapeDtypeStruct((m, d), F32),
                   jax.ShapeDtypeStruct((m, d), BF16)],
        scratch_shapes=[pltpu.VMEM((tm, d), F32), pltpu.VMEM((tm, d), F32)],
        compiler_params=_params(("arbitrary", "arbitrary"), vm),
        name=name,
    )(a, w, x, gpost, gnext)


def _softplus_neg(lam):
    return jnp.maximum(-lam, 0.0) + jnp.log1p(jnp.exp(-jnp.abs(lam)))


def _rglru_coeffs(xc, wr_ref, wi_ref, br, bi, sp):
    xcb = xc.astype(BF16)
    hbr, hbi = 0.5 * br, 0.5 * bi
    c = (-0.5 * RG_C) * sp
    a_parts, b_parts = [], []
    for g in range(xc.shape[1] // RG_BLK):
        sl = slice(g * RG_BLK, (g + 1) * RG_BLK)
        xg = xcb[:, sl]
        tr = jnp.tanh(jnp.dot(xg, (0.5 * wr_ref[g]).astype(BF16),
                              preferred_element_type=F32) + hbr[:, sl])
        ti = jnp.tanh(jnp.dot(xg, (0.5 * wi_ref[g]).astype(BF16),
                              preferred_element_type=F32) + hbi[:, sl])
        log_a = c[:, sl] * tr + c[:, sl]
        a = jnp.exp(log_a)
        y = jnp.tanh(log_a) * (-1.0 - a * a)
        mult = jnp.where(y > 0.0, y * lax.rsqrt(y), 0.0)
        a_parts.append(a)
        b_parts.append((mult * xc[:, sl]) * (0.5 * ti + 0.5))
    if len(a_parts) == 1:
        return a_parts[0], b_parts[0]
    return jnp.concatenate(a_parts, axis=1), jnp.concatenate(b_parts, axis=1)


def _conv_taps(taps, w, width):
    acc = taps[0] * w[0:1, :]
    for k in range(1, width):
        acc = acc + taps[k] * w[k:k + 1, :]
    return acc


def _scan8(a, b):
    row = lax.broadcasted_iota(jnp.int32, a.shape, 0)
    for s in (1, 2, 4):
        keep = row >= s
        b = jnp.where(keep, a * pltpu.roll(b, s, 0) + b, b)
        a = jnp.where(keep, a * pltpu.roll(a, s, 0), a)
    return a, b


def _prev_segment(grp):
    row = lax.broadcasted_iota(jnp.int32, grp.shape, 0)
    return jnp.where(row == 0, 0.0, pltpu.roll(grp, 1, 0))


def _mix_a_prompt_kernel(x_ref, g_ref, w4_ref, b4_ref, wr_ref, wi_ref, br_ref, bi_ref,
                         lam_ref, u_ref, hn_ref, cn_ref, xh_ref, hh_ref, pp_ref,
                         *, seq, rc):
    tc = x_ref.shape[1]
    halo = (CONV4_W - 1) * SUBLANES
    w4 = w4_ref[...]
    b4 = b4_ref[...]
    br = br_ref[...]
    bi = bi_ref[...]
    sp = _softplus_neg(lam_ref[...])

    for k in range(CONV4_W - 1):
        grp = x_ref[seq - halo + SUBLANES * k:seq - halo + SUBLANES * (k + 1), :]
        xh_ref[SUBLANES * k:SUBLANES * (k + 1), :] = _prev_segment(grp)

    h = jnp.zeros((SUBLANES, tc), F32)
    p = jnp.ones((SUBLANES, tc), F32)
    for c in range(seq // rc):
        r0 = c * rc
        if c == 0:
            xh_ref[halo:halo + rc, :] = x_ref[0:rc, :]
            taps = [xh_ref[SUBLANES * k:SUBLANES * k + rc, :] for k in range(CONV4_W)]
        else:
            taps = [x_ref[r0 - halo + SUBLANES * k:r0 - halo + SUBLANES * k + rc, :]
                    for k in range(CONV4_W)]
        xc = _conv_taps(taps, w4, CONV4_W) + b4
        a, b = _rglru_coeffs(xc, wr_ref, wi_ref, br, bi, sp)
        for q in range(rc // SUBLANES):
            aq = a[SUBLANES * q:SUBLANES * (q + 1), :]
            h = aq * h + b[SUBLANES * q:SUBLANES * (q + 1), :]
            p = aq * p
            hh_ref[r0 + SUBLANES * q:r0 + SUBLANES * (q + 1), :] = h
            pp_ref[r0 + SUBLANES * q:r0 + SUBLANES * (q + 1), :] = p

    _, hc = _scan8(p, h)
    cin = _prev_segment(hc)
    hn_ref[0] = hc[SUBLANES - 1:SUBLANES, :]

    for c in range(seq // rc):
        r0 = c * rc
        cin_t = jnp.tile(cin, (rc // SUBLANES, 1))
        hfull = hh_ref[r0:r0 + rc, :] + pp_ref[r0:r0 + rc, :] * cin_t
        u_ref[r0:r0 + rc, :] = (_gelu(g_ref[r0:r0 + rc, :]) * hfull).astype(u_ref.dtype)

    for k in range(CONV4_W - 1):
        r = seq - halo + SUBLANES * k + SUBLANES - 1
        cn_ref[0, k:k + 1, :] = x_ref[r:r + 1, :]


def _mix_a_sample_kernel(x_ref, g_ref, h0_ref, c4_ref, w4_ref, b4_ref, wr_ref, wi_ref,
                         br_ref, bi_ref, lam_ref, u_ref, hn_ref, cn_ref, xh_ref,
                         *, nseq, seq):
    halo = (CONV4_W - 1) * nseq
    m = nseq * seq
    sp = _softplus_neg(lam_ref[...])
    xh_ref[0:halo, :] = c4_ref[...]
    xh_ref[halo:halo + m, :] = x_ref[...]
    taps = [xh_ref[nseq * k:nseq * k + m, :] for k in range(CONV4_W)]
    xc = _conv_taps(taps, w4_ref[...], CONV4_W) + b4_ref[...]
    a, b = _rglru_coeffs(xc, wr_ref, wi_ref, br_ref[...], bi_ref[...], sp)
    h = h0_ref[...]
    for t in range(seq):
        rows = slice(nseq * t, nseq * (t + 1))
        h = a[rows, :] * h + b[rows, :]
        u_ref[rows, :] = (_gelu(g_ref[rows, :]) * h).astype(u_ref.dtype)
    hn_ref[...] = h
    cn_ref[...] = xh_ref[m:m + halo, :]


def _mix_b_prompt_kernel(sb_ref, sc_ref, sx_ref, w_ref, u_ref, sn_ref, uh_ref, *, seq, rc):
    halo = (SC_W - 1) * SUBLANES
    w = w_ref[...]
    for c in range(seq // rc):
        r0 = c * rc
        uh_ref[halo + r0:halo + r0 + rc, :] = sc_ref[r0:r0 + rc, :] * sx_ref[r0:r0 + rc, :]
    for k in range(SC_W - 1):
        grp = uh_ref[seq + SUBLANES * k:seq + SUBLANES * (k + 1), :]
        uh_ref[SUBLANES * k:SUBLANES * (k + 1), :] = _prev_segment(grp)
    for c in range(seq // rc):
        r0 = c * rc
        taps = [uh_ref[r0 + SUBLANES * k:r0 + SUBLANES * k + rc, :] for k in range(SC_W)]
        uc = _conv_taps(taps, w, SC_W)
        u_ref[r0:r0 + rc, :] = (sb_ref[r0:r0 + rc, :] * uc).astype(u_ref.dtype)
    for k in range(SC_W - 1):
        r = seq + SUBLANES * k + SUBLANES - 1
        sn_ref[0, k:k + 1, :] = uh_ref[r:r + 1, :]


def _mix_b_sample_kernel(sb_ref, sc_ref, sx_ref, buf_ref, w_ref, u_ref, sn_ref, uh_ref,
                         *, nseq, seq):
    halo = (SC_W - 1) * nseq
    m = nseq * seq
    uh_ref[0:halo, :] = buf_ref[...]
    uh_ref[halo:halo + m, :] = sc_ref[...] * sx_ref[...]
    taps = [uh_ref[nseq * k:nseq * k + m, :] for k in range(SC_W)]
    uc = _conv_taps(taps, w_ref[...], SC_W)
    u_ref[...] = (sb_ref[...] * uc).astype(u_ref.dtype)
    sn_ref[...] = uh_ref[m:m + halo, :]


def _mix_prompt_kernel(x_ref, g_ref, w4_ref, b4_ref, wr_ref, wi_ref, br_ref, bi_ref, lam_ref,
                       sb_ref, sc_ref, sx_ref, wsc_ref,
                       ua_ref, hn_ref, cn_ref, ub_ref, sn_ref,
                       xh_ref, hh_ref, pp_ref, uh_ref, *, seq, rc):
    _mix_b_prompt_kernel(sb_ref, sc_ref, sx_ref, wsc_ref, ub_ref, sn_ref, uh_ref,
                         seq=seq, rc=rc)
    _mix_a_prompt_kernel(x_ref, g_ref, w4_ref, b4_ref, wr_ref, wi_ref, br_ref, bi_ref,
                         lam_ref, ua_ref, hn_ref, cn_ref, xh_ref, hh_ref, pp_ref,
                         seq=seq, rc=rc)


def _mix_prompt(z, offs, l, w4, b4, wr, wi, br, bi, lam, wsc, batch, seq, tc=256, rc=256):
    da, db = w4.shape[1], wsc.shape[1]
    tc, rc = min(tc, da), min(rc, seq)
    nj = da // tc
    tb = db // nj
    vec = pl.BlockSpec((1, tc), lambda j, b: (0, j))
    blk = pl.BlockSpec((None, tc // RG_BLK, RG_BLK, RG_BLK), lambda j, b: (l, j, 0, 0))
    vm = (2 * (2 * seq * tc * 4 + seq * tc * 2 + 2 * tc * RG_BLK * 4 + 3 * seq * tb * 4
               + seq * tb * 2) + 3 * seq * tc * 4 + (seq + 16) * tb * 4)

    def cola(off):
        return pl.BlockSpec((seq, tc), lambda j, b: (b, off // tc + j))

    def colb(off):
        return pl.BlockSpec((seq, tb), lambda j, b: (b, off // tb + j))

    return pl.pallas_call(
        functools.partial(_mix_prompt_kernel, seq=seq, rc=rc),
        grid=(nj, batch),
        in_specs=[cola(offs["x"]), cola(offs["g"]),
                  pl.BlockSpec((CONV4_W, tc), lambda j, b: (0, j)),
                  vec, blk, blk, vec, vec, vec,
                  colb(offs["sb"]), colb(offs["sc"]), colb(offs["sx"]),
                  pl.BlockSpec((SC_W, tb), lambda j, b: (0, j))],
        out_specs=[pl.BlockSpec((seq, tc), lambda j, b: (b, j)),
                   pl.BlockSpec((1, 1, tc), lambda j, b: (b, 0, j)),
                   pl.BlockSpec((1, CONV4_W - 1, tc), lambda j, b: (b, 0, j)),
                   pl.BlockSpec((seq, tb), lambda j, b: (b, j)),
                   pl.BlockSpec((1, SC_W - 1, tb), lambda j, b: (b, 0, j))],
        out_shape=[jax.ShapeDtypeStruct((batch * seq, da), BF16),
                   jax.ShapeDtypeStruct((batch, 1, da), F32),
                   jax.ShapeDtypeStruct((batch, CONV4_W - 1, da), F32),
                   jax.ShapeDtypeStruct((batch * seq, db), BF16),
                   jax.ShapeDtypeStruct((batch, SC_W - 1, db), F32)],
        scratch_shapes=[pltpu.VMEM(((CONV4_W - 1) * SUBLANES + rc, tc), F32),
                        pltpu.VMEM((seq, tc), F32),
                        pltpu.VMEM((seq, tc), F32),
                        pltpu.VMEM(((SC_W - 1) * SUBLANES + seq, tb), F32)],
        compiler_params=_params(("parallel", "arbitrary"), vm),
        name="mix_prompt",
    )(z, z, w4, b4, wr, wi, br, bi, lam, z, z, z, wsc)


def _mix_sample_kernel(x_ref, g_ref, h0_ref, c4_ref, w4_ref, b4_ref, wr_ref, wi_ref, br_ref,
                       bi_ref, lam_ref, sb_ref, sc_ref, sx_ref, buf_ref, wsc_ref,
                       ua_ref, hn_ref, cn_ref, ub_ref, sn_ref, xh_ref, uh_ref, *, nseq, seq):
    _mix_b_sample_kernel(sb_ref, sc_ref, sx_ref, buf_ref, wsc_ref, ub_ref, sn_ref, uh_ref,
                         nseq=nseq, seq=seq)
    _mix_a_sample_kernel(x_ref, g_ref, h0_ref, c4_ref, w4_ref, b4_ref, wr_ref, wi_ref,
                         br_ref, bi_ref, lam_ref, ua_ref, hn_ref, cn_ref, xh_ref,
                         nseq=nseq, seq=seq)


def _mix_sample(z, offs, h0, c4, buf, l, w4, b4, wr, wi, br, bi, lam, wsc, nseq, seq, tc=256):
    da, db = w4.shape[1], wsc.shape[1]
    tc = min(tc, da)
    nj = da // tc
    tb = db // nj
    m = nseq * seq
    ha, hb = (CONV4_W - 1) * nseq, (SC_W - 1) * nseq
    vec = pl.BlockSpec((1, tc), lambda j: (0, j))
    blk = pl.BlockSpec((None, tc // RG_BLK, RG_BLK, RG_BLK), lambda j: (l, j, 0, 0))
    vm = (2 * (2 * m * tc * 4 + m * tc * 2 + nseq * tc * 8 + 2 * ha * tc * 4
               + 2 * tc * RG_BLK * 4 + 3 * m * tb * 4 + m * tb * 2 + 2 * hb * tb * 4)
          + 8 * m * tc * 4 + 4 * m * tb * 4)

    def cola(off):
        return pl.BlockSpec((m, tc), lambda j: (0, off // tc + j))

    def colb(off):
        return pl.BlockSpec((m, tb), lambda j: (0, off // tb + j))

    return pl.pallas_call(
        functools.partial(_mix_sample_kernel, nseq=nseq, seq=seq),
        grid=(nj,),
        in_specs=[cola(offs["x"]), cola(offs["g"]),
                  pl.BlockSpec((nseq, tc), lambda j: (0, j)),
                  pl.BlockSpec((ha, tc), lambda j: (0, j)),
                  pl.BlockSpec((CONV4_W, tc), lambda j: (0, j)),
                  vec, blk, blk, vec, vec, vec,
                  colb(offs["sb"]), colb(offs["sc"]), colb(offs["sx"]),
                  pl.BlockSpec((hb, tb), lambda j: (0, j)),
                  pl.BlockSpec((SC_W, tb), lambda j: (0, j))],
        out_specs=[pl.BlockSpec((m, tc), lambda j: (0, j)),
                   pl.BlockSpec((nseq, tc), lambda j: (0, j)),
                   pl.BlockSpec((ha, tc), lambda j: (0, j)),
                   pl.BlockSpec((m, tb), lambda j: (0, j)),
                   pl.BlockSpec((hb, tb), lambda j: (0, j))],
        out_shape=[jax.ShapeDtypeStruct((m, da), BF16),
                   jax.ShapeDtypeStruct((nseq, da), F32),
                   jax.ShapeDtypeStruct((ha, da), F32),
                   jax.ShapeDtypeStruct((m, db), BF16),
                   jax.ShapeDtypeStruct((hb, db), F32)],
        scratch_shapes=[pltpu.VMEM((ha + m, tc), F32),
                        pltpu.VMEM((hb + m, tb), F32)],
        compiler_params=_params(("parallel",), vm),
        name="mix_sample",
    )(z, z, h0, c4, w4, b4, wr, wi, br, bi, lam, z, z, z, buf, wsc)


def _time_major(s):
    b, w, c = s.shape
    return jnp.swapaxes(s, 0, 1).reshape(w * b, c)


def _batch_major(s, b):
    return jnp.swapaxes(s.reshape(-1, b, s.shape[1]), 0, 1)


def _layer(x, h, l, lw, gnext, group):
    d_rnn = lw["w_conv4"].shape[1]
    d_conv = lw["w_sc"].shape[1]
    off_x, off_g = 0, d_rnn
    off_sb = 2 * d_rnn
    off_sc = off_sb + d_conv
    off_sx = off_sc + d_conv
    off_ga = off_sx + d_conv
    off_gb = off_ga + x.shape[1]
    nb, seq = group["batch"], group["seq"]

    offs = {"x": off_x, "g": off_g, "sb": off_sb, "sc": off_sc, "sx": off_sx}
    z = _mm(h, lw["w_in"], l, F32, False, "mm_in")
    mixw = (l, lw["w_conv4"], lw["b_conv4"], lw["w_rg_r"], lw["w_rg_i"], lw["b_rg_r"],
            lw["b_rg_i"], lw["rg_lambda"], lw["w_sc"])
    if group["kind"] == "prompt":
        ua, hn, cn, ub, sn = _mix_prompt(z, offs, *mixw, nb, seq)
        hn = hn.reshape(nb, d_rnn)
    else:
        ua, hn, cn, ub, sn = _mix_sample(z, offs, group["h0"], _time_major(group["c4"]),
                                         _time_major(group["sc"]), *mixw, nb, seq)
        cn = _batch_major(cn, nb)
        sn = _batch_major(sn, nb)
    pp = group["kind"] == "prompt"
    mm_norm = _mm_norm_pp if pp else _mm_norm
    m = _merge(ua, ub, lw["w_out_a"], lw["w_out_b"], l, z, off_ga, off_gb)
    x1, hm = mm_norm(m, lw["w_o"], l, x, lw["norm_mix_post"], lw["norm_mlp_pre"], "mm_o")
    a = _mm(hm, lw["w_mlp_up"], l, BF16, True, "mm_up", tiled_out=True)
    x2, hnext = mm_norm(a, lw["w_mlp_down"], l, x1, lw["norm_mlp_post"], gnext, "mm_down")
    return x2, hnext, hn, cn, sn


def kernel(x_prompt, x_sample, state_rglru_h, state_conv4, state_shortconv, norm_mix_pre,
           norm_mix_post, w_in, w_conv4, b_conv4, w_rg_r, b_rg_r, w_rg_i, b_rg_i, rg_lambda,
           w_out_a, w_sc, w_out_b, w_o, norm_mlp_pre, norm_mlp_post, w_mlp_up, w_mlp_down):
    depth = w_in.shape[0]
    bp, sp, d = x_prompt.shape
    bs, ss, _ = x_sample.shape
    assert sp % (SUBLANES * SUBLANES) == 0 and bs % SUBLANES == 0

    xp = jnp.swapaxes(x_prompt.reshape(bp, SUBLANES, sp // SUBLANES, d), 1, 2)
    xp = xp.reshape(bp * sp, d)
    xs = jnp.swapaxes(x_sample, 0, 1).reshape(ss * bs, d)
    hp = _norm(xp, norm_mix_pre[0].reshape(1, d))
    hs = _norm(xs, norm_mix_pre[0].reshape(1, d))

    outs_p, outs_s = [], []
    for l in range(depth):
        lw = {
            "w_in": w_in,
            "w_conv4": w_conv4[l],
            "b_conv4": b_conv4[l].reshape(1, -1),
            "w_rg_r": w_rg_r,
            "w_rg_i": w_rg_i,
            "b_rg_r": b_rg_r[l].reshape(1, -1),
            "b_rg_i": b_rg_i[l].reshape(1, -1),
            "rg_lambda": rg_lambda[l].reshape(1, -1),
            "w_out_a": w_out_a,
            "w_sc": w_sc[l],
            "w_out_b": w_out_b,
            "w_o": w_o,
            "norm_mix_post": norm_mix_post[l].reshape(1, -1),
            "norm_mlp_pre": norm_mlp_pre[l].reshape(1, -1),
            "norm_mlp_post": norm_mlp_post[l].reshape(1, -1),
            "w_mlp_up": w_mlp_up,
            "w_mlp_down": w_mlp_down,
        }
        gnext = norm_mix_pre[(l + 1) % depth].reshape(1, -1)
        gp = {"kind": "prompt", "batch": bp, "seq": sp}
        gs = {"kind": "sample", "batch": bs, "seq": ss, "h0": state_rglru_h[l],
              "c4": state_conv4[l], "sc": state_shortconv[l]}
        xp, hp, hnp, cnp, snp = _layer(xp, hp, l, lw, gnext, gp)
        xs, hs, hns, cns, sns = _layer(xs, hs, l, lw, gnext, gs)
        outs_p.append((hnp, cnp, snp))
        outs_s.append((hns, cns, sns))

    yp = jnp.swapaxes(xp.reshape(bp, sp // SUBLANES, SUBLANES, d), 1, 2).reshape(bp, sp, d)
    ys = jnp.swapaxes(xs.reshape(ss, bs, d), 0, 1)
    return (yp, ys,
            jnp.stack([o[0] for o in outs_p]), jnp.stack([o[1] for o in outs_p]),
            jnp.stack([o[2] for o in outs_p]),
            jnp.stack([o[0] for o in outs_s]), jnp.stack([o[1] for o in outs_s]),
            jnp.stack([o[2] for o in outs_s]))
```

```python
import functools

import jax
import jax.numpy as jnp
from jax import lax
from jax.experimental import pallas as pl
from jax.experimental.pallas import tpu as pltpu

RG_BLK = 256
RG_C = 8.0
CONV4_W = 4
SC_W = 3
EPS = 1e-6

SUBLANES = 8
V7X_VMEM_BUDGET = 58 * 1024 * 1024

F32 = jnp.float32
BF16 = jnp.bfloat16


def _params(semantics, vmem_bytes):
    return pltpu.CompilerParams(
        dimension_semantics=semantics,
        vmem_limit_bytes=min(int(vmem_bytes) + (12 << 20), V7X_VMEM_BUDGET),
    )


def _rms(x, g):
    ms = jnp.mean(x * x, axis=-1, keepdims=True)
    return (x * lax.rsqrt(ms + EPS)) * g


def _sigmoid(x):
    return 0.5 * jnp.tanh(0.5 * x) + 0.5


GELU_C = 0.7978845608028654
GELU_C3 = GELU_C * 0.044715


def _gelu(x):
    hx = 0.5 * x
    return hx * jnp.tanh(x * (GELU_C + GELU_C3 * (x * x))) + hx


def _norm_kernel(x_ref, g_ref, o_ref):
    o_ref[...] = _rms(x_ref[...], g_ref[...]).astype(o_ref.dtype)


def _norm(x, g, tm=256):
    m, d = x.shape
    tm = min(tm, m)
    return pl.pallas_call(
        _norm_kernel,
        grid=(m // tm,),
        in_specs=[pl.BlockSpec((tm, d), lambda i: (i, 0)),
                  pl.BlockSpec((1, d), lambda i: (0, 0))],
        out_specs=pl.BlockSpec((tm, d), lambda i: (i, 0)),
        out_shape=jax.ShapeDtypeStruct((m, d), BF16),
        compiler_params=_params(("parallel",), 2 * tm * d * 6),
        name="norm0",
    )(x, g)


def _mm_kernel(x_ref, w_ref, o_ref, *, relu2):
    acc = jnp.dot(x_ref[...], w_ref[...].astype(BF16), preferred_element_type=F32)
    if relu2:
        acc = jnp.square(jnp.maximum(acc, 0.0))
    o_ref[...] = acc.astype(o_ref.dtype)


def _mm(x, w, l, out_dtype, relu2, name, tm=2048, tn=512, tiled_out=False):
    m, k = x.shape
    n = w.shape[2]
    tm, tn = min(tm, m), min(tn, n)
    if tiled_out:
        out_spec = pl.BlockSpec((None, tm, tn), lambda i, j: (j, i, 0))
        out_shape = jax.ShapeDtypeStruct((n // tn, m, tn), out_dtype)
    else:
        out_spec = pl.BlockSpec((tm, tn), lambda i, j: (i, j))
        out_shape = jax.ShapeDtypeStruct((m, n), out_dtype)
    vm = (tm * k * 2 + 2 * (k * tn * 4 + tm * tn * jnp.dtype(out_dtype).itemsize)
          + k * tn * 2 + tm * tn * 4)
    return pl.pallas_call(
        functools.partial(_mm_kernel, relu2=relu2),
        grid=(m // tm, n // tn),
        in_specs=[pl.BlockSpec((tm, k), lambda i, j: (i, 0), pipeline_mode=pl.Buffered(1)),
                  pl.BlockSpec((None, k, tn), lambda i, j: (l, 0, j))],
        out_specs=out_spec,
        out_shape=out_shape,
        compiler_params=_params(("parallel", "arbitrary"), vm),
        name=name,
    )(x, w)


def _merge_kernel(ua_ref, ub_ref, wa_ref, wb_ref, ga_ref, gb_ref, o_ref):
    ya = jnp.dot(ua_ref[...], wa_ref[...].astype(BF16), preferred_element_type=F32)
    yb = jnp.dot(ub_ref[...], wb_ref[...].astype(BF16), preferred_element_type=F32)
    m = _sigmoid(ga_ref[...]) * ya + _sigmoid(gb_ref[...]) * yb
    o_ref[...] = m.astype(o_ref.dtype)


def _merge(ua, ub, wa, wb, l, z, off_ga, off_gb, tm=1024, tn=256):
    m, ka = ua.shape
    kb = ub.shape[1]
    n = wa.shape[2]
    tm, tn = min(tm, m), min(tn, n)
    vm = (2 * (tm * ka * 2 + tm * kb * 2 + ka * tn * 4 + kb * tn * 4
               + 2 * tm * tn * 4 + tm * tn * 2)
          + (ka + kb) * tn * 2 + 2 * tm * tn * 4)
    return pl.pallas_call(
        _merge_kernel,
        grid=(m // tm, n // tn),
        in_specs=[pl.BlockSpec((tm, ka), lambda i, j: (i, 0)),
                  pl.BlockSpec((tm, kb), lambda i, j: (i, 0)),
                  pl.BlockSpec((None, ka, tn), lambda i, j: (l, 0, j)),
                  pl.BlockSpec((None, kb, tn), lambda i, j: (l, 0, j)),
                  pl.BlockSpec((tm, tn), lambda i, j: (i, off_ga // tn + j)),
                  pl.BlockSpec((tm, tn), lambda i, j: (i, off_gb // tn + j))],
        out_specs=pl.BlockSpec((None, tm, tn), lambda i, j: (j, i, 0)),
        out_shape=jax.ShapeDtypeStruct((n // tn, m, tn), BF16),
        compiler_params=_params(("parallel", "arbitrary"), vm),
        name="merge",
    )(ua, ub, wa, wb, z, z)


def _mm_norm_kernel(a_ref, w_ref, x_ref, gpost_ref, gnext_ref, xo_ref, ho_ref, acc_ref,
                    *, nk, te):
    k = pl.program_id(1)

    @pl.when(k == 0)
    def _():
        acc_ref[...] = jnp.dot(a_ref[...], w_ref[...].astype(BF16),
                               preferred_element_type=F32)

    @pl.when(jnp.logical_and(k > 0, k < nk))
    def _():
        acc_ref[...] += jnp.dot(a_ref[...], w_ref[...].astype(BF16),
                                preferred_element_type=F32)

    @pl.when(k >= nk)
    def _():
        r0 = pl.multiple_of((k - nk) * te, te)
        o = acc_ref[pl.ds(r0, te), :]
        x1 = x_ref[...] + _rms(o, gpost_ref[...])
        xo_ref[...] = x1
        ho_ref[...] = _rms(x1, gnext_ref[...]).astype(ho_ref.dtype)


def _mm_norm(a, w, l, x, gpost, gnext, name, tm=1024, te=128):
    nk, m, tk = a.shape
    d = w.shape[2]
    tm = min(tm, m)
    te = min(te, tm)
    ne = tm // te
    vm = (2 * (tm * tk * 2 + tk * d * 4 + te * d * 4 + te * d * 4 + te * d * 2)
          + tm * d * 4 + tk * d * 2)

    def slab(i, k):
        return (i * ne + jnp.maximum(k - nk, 0), 0)

    return pl.pallas_call(
        functools.partial(_mm_norm_kernel, nk=nk, te=te),
        grid=(m // tm, nk + ne),
        in_specs=[pl.BlockSpec((None, tm, tk), lambda i, k: (jnp.minimum(k, nk - 1), i, 0)),
                  pl.BlockSpec((None, tk, d), lambda i, k: (l, jnp.minimum(k, nk - 1), 0)),
                  pl.BlockSpec((te, d), slab),
                  pl.BlockSpec((1, d), lambda i, k: (0, 0)),
                  pl.BlockSpec((1, d), lambda i, k: (0, 0))],
        out_specs=[pl.BlockSpec((te, d), slab),
                   pl.BlockSpec((te, d), slab)],
        out_shape=[jax.ShapeDtypeStruct((m, d), F32),
                   jax.ShapeDtypeStruct((m, d), BF16)],
        scratch_shapes=[pltpu.VMEM((tm, d), F32)],
        compiler_params=_params(("parallel", "arbitrary"), vm),
        name=name,
    )(a, w, x, gpost, gnext)


def _mm_norm_pp_kernel(a_ref, w_ref, x_ref, gpost_ref, gnext_ref, xo_ref, ho_ref,
                       acc0_ref, acc1_ref, *, ni, te, grp):
    p = pl.program_id(0)
    k = pl.program_id(1)
    accs = (acc0_ref, acc1_ref)

    def contract(acc_ref):
        acc_ref[...] += jnp.dot(a_ref[...], w_ref[...].astype(BF16),
                                preferred_element_type=F32)

    def drain(acc_ref):
        r0 = pl.multiple_of(k * te, te)
        sub = pl.ds(pl.multiple_of((k % grp) * te, te), te)
        o = acc_ref[pl.ds(r0, te), :]
        acc_ref[pl.ds(r0, te), :] = jnp.zeros_like(o)
        x1 = x_ref[sub, :] + _rms(o, gpost_ref[...])
        xo_ref[sub, :] = x1
        ho_ref[sub, :] = _rms(x1, gnext_ref[...]).astype(ho_ref.dtype)

    @pl.when(p == 0)
    def _():
        @pl.when(k == 0)
        def _():
            acc0_ref[...] = jnp.zeros(acc0_ref.shape, F32)
            acc1_ref[...] = jnp.zeros(acc1_ref.shape, F32)
        contract(acc0_ref)

    for parity in (0, 1):
        @pl.when(jnp.logical_and(jnp.logical_and(p > 0, p < ni), p % 2 == parity))
        def _():
            contract(accs[parity])
            drain(accs[1 - parity])

    @pl.when(p == ni)
    def _():
        drain(accs[(ni - 1) % 2])


def _mm_norm_pp(a, w, l, x, gpost, gnext, name, tm=1024, grp=2):
    nk, m, tk = a.shape
    d = w.shape[2]
    tm = min(tm, m // 2)
    ni = m // tm
    te = tm // nk
    grp = min(grp, nk)
    tb = te * grp
    assert te % 16 == 0 and nk % grp == 0
    vm = (2 * (tm * tk * 2 + tk * d * 4 + tb * d * 4 + tb * d * 4 + tb * d * 2)
          + 2 * tm * d * 4 + tk * d * 2)

    def kk(p, k):
        return jnp.where(p < ni, k, nk - 1)

    def slab(p, k):
        return (jnp.where(p > 0, (p - 1) * (nk // grp) + k // grp, 0), 0)

    return pl.pallas_call(
        functools.partial(_mm_norm_pp_kernel, ni=ni, te=te, grp=grp),
        grid=(ni + 1, nk),
        in_specs=[pl.BlockSpec((None, tm, tk),
                               lambda p, k: (kk(p, k), jnp.minimum(p, ni - 1), 0)),
                  pl.BlockSpec((None, tk, d), lambda p, k: (l, kk(p, k), 0)),
                  pl.BlockSpec((tb, d), slab),
                  pl.BlockSpec((1, d), lambda p, k: (0, 0)),
                  pl.BlockSpec((1, d), lambda p, k: (0, 0))],
        out_specs=[pl.BlockSpec((tb, d), slab),
                   pl.BlockSpec((tb, d), slab)],
        out_shape=[jax.ShapeDtypeStruct((m, d), F32),
                   jax.ShapeDtypeStruct((m, d), BF16)],
        scratch_shapes=[pltpu.VMEM((tm, d), F32), pltpu.VMEM((tm, d), F32)],
        compiler_params=_params(("arbitrary", "arbitrary"), vm),
        name=name,
    )(a, w, x, gpost, gnext)


def _softplus_neg(lam):
    return jnp.maximum(-lam, 0.0) + jnp.log1p(jnp.exp(-jnp.abs(lam)))


def _rglru_coeffs(xc, wr_ref, wi_ref, br, bi, sp):
    xcb = xc.astype(BF16)
    hbr, hbi = 0.5 * br, 0.5 * bi
    c = (-0.5 * RG_C) * sp
    a_parts, b_parts = [], []
    for g in range(xc.shape[1] // RG_BLK):
        sl = slice(g * RG_BLK, (g + 1) * RG_BLK)
        xg = xcb[:, sl]
        tr = jnp.tanh(jnp.dot(xg, (0.5 * wr_ref[g]).astype(BF16),
                              preferred_element_type=F32) + hbr[:, sl])
        ti = jnp.tanh(jnp.dot(xg, (0.5 * wi_ref[g]).astype(BF16),
                              preferred_element_type=F32) + hbi[:, sl])
        log_a = c[:, sl] * tr + c[:, sl]
        a = jnp.exp(log_a)
        y = jnp.tanh(log_a) * (-1.0 - a * a)
        mult = jnp.where(y > 0.0, y * lax.rsqrt(y), 0.0)
        a_parts.append(a)
        b_parts.append((mult * xc[:, sl]) * (0.5 * ti + 0.5))
    if len(a_parts) == 1:
        return a_parts[0], b_parts[0]
    return jnp.concatenate(a_parts, axis=1), jnp.concatenate(b_parts, axis=1)


def _conv_taps(taps, w, width):
    acc = taps[0] * w[0:1, :]
    for k in range(1, width):
        acc = acc + taps[k] * w[k:k + 1, :]
    return acc


def _scan8(a, b):
    row = lax.broadcasted_iota(jnp.int32, a.shape, 0)
    for s in (1, 2, 4):
        keep = row >= s
        b = jnp.where(keep, a * pltpu.roll(b, s, 0) + b, b)
        a = jnp.where(keep, a * pltpu.roll(a, s, 0), a)
    return a, b


def _prev_segment(grp):
    row = lax.broadcasted_iota(jnp.int32, grp.shape, 0)
    return jnp.where(row == 0, 0.0, pltpu.roll(grp, 1, 0))


def _mix_a_prompt_kernel(x_ref, g_ref, w4_ref, b4_ref, wr_ref, wi_ref, br_ref, bi_ref,
                         lam_ref, u_ref, hn_ref, cn_ref, xh_ref, hh_ref, pp_ref,
                         *, seq, rc):
    tc = x_ref.shape[1]
    halo = (CONV4_W - 1) * SUBLANES
    w4 = w4_ref[...]
    b4 = b4_ref[...]
    br = br_ref[...]
    bi = bi_ref[...]
    sp = _softplus_neg(lam_ref[...])

    for k in range(CONV4_W - 1):
        grp = x_ref[seq - halo + SUBLANES * k:seq - halo + SUBLANES * (k + 1), :]
        xh_ref[SUBLANES * k:SUBLANES * (k + 1), :] = _prev_segment(grp)

    h = jnp.zeros((SUBLANES, tc), F32)
    p = jnp.ones((SUBLANES, tc), F32)
    for c in range(seq // rc):
        r0 = c * rc
        if c == 0:
            xh_ref[halo:halo + rc, :] = x_ref[0:rc, :]
            taps = [xh_ref[SUBLANES * k:SUBLANES * k + rc, :] for k in range(CONV4_W)]
        else:
            taps = [x_ref[r0 - halo + SUBLANES * k:r0 - halo + SUBLANES * k + rc, :]
                    for k in range(CONV4_W)]
        xc = _conv_taps(taps, w4, CONV4_W) + b4
        a, b = _rglru_coeffs(xc, wr_ref, wi_ref, br, bi, sp)
        for q in range(rc // SUBLANES):
            aq = a[SUBLANES * q:SUBLANES * (q + 1), :]
            h = aq * h + b[SUBLANES * q:SUBLANES * (q + 1), :]
            p = aq * p
            hh_ref[r0 + SUBLANES * q:r0 + SUBLANES * (q + 1), :] = h
            pp_ref[r0 + SUBLANES * q:r0 + SUBLANES * (q + 1), :] = p

    _, hc = _scan8(p, h)
    cin = _prev_segment(hc)
    hn_ref[0] = hc[SUBLANES - 1:SUBLANES, :]

    for c in range(seq // rc):
        r0 = c * rc
        cin_t = jnp.tile(cin, (rc // SUBLANES, 1))
        hfull = hh_ref[r0:r0 + rc, :] + pp_ref[r0:r0 + rc, :] * cin_t
        u_ref[r0:r0 + rc, :] = (_gelu(g_ref[r0:r0 + rc, :]) * hfull).astype(u_ref.dtype)

    for k in range(CONV4_W - 1):
        r = seq - halo + SUBLANES * k + SUBLANES - 1
        cn_ref[0, k:k + 1, :] = x_ref[r:r + 1, :]


def _mix_a_sample_kernel(x_ref, g_ref, h0_ref, c4_ref, w4_ref, b4_ref, wr_ref, wi_ref,
                         br_ref, bi_ref, lam_ref, u_ref, hn_ref, cn_ref, xh_ref,
                         *, nseq, seq):
    halo = (CONV4_W - 1) * nseq
    m = nseq * seq
    sp = _softplus_neg(lam_ref[...])
    xh_ref[0:halo, :] = c4_ref[...]
    xh_ref[halo:halo + m, :] = x_ref[...]
    taps = [xh_ref[nseq * k:nseq * k + m, :] for k in range(CONV4_W)]
    xc = _conv_taps(taps, w4_ref[...], CONV4_W) + b4_ref[...]
    a, b = _rglru_coeffs(xc, wr_ref, wi_ref, br_ref[...], bi_ref[...], sp)
    h = h0_ref[...]
    for t in range(seq):
        rows = slice(nseq * t, nseq * (t + 1))
        h = a[rows, :] * h + b[rows, :]
        u_ref[rows, :] = (_gelu(g_ref[rows, :]) * h).astype(u_ref.dtype)
    hn_ref[...] = h
    cn_ref[...] = xh_ref[m:m + halo, :]


def _mix_b_prompt_kernel(sb_ref, sc_ref, sx_ref, w_ref, u_ref, sn_ref, uh_ref, *, seq, rc):
    halo = (SC_W - 1) * SUBLANES
    w = w_ref[...]
    for c in range(seq // rc):
        r0 = c * rc
        uh_ref[halo + r0:halo + r0 + rc, :] = sc_ref[r0:r0 + rc, :] * sx_ref[r0:r0 + rc, :]
    for k in range(SC_W - 1):
        grp = uh_ref[seq + SUBLANES * k:seq + SUBLANES * (k + 1), :]
        uh_ref[SUBLANES * k:SUBLANES * (k + 1), :] = _prev_segment(grp)
    for c in range(seq // rc):
        r0 = c * rc
        taps = [uh_ref[r0 + SUBLANES * k:r0 + SUBLANES * k + rc, :] for k in range(SC_W)]
        uc = _conv_taps(taps, w, SC_W)
        u_ref[r0:r0 + rc, :] = (sb_ref[r0:r0 + rc, :] * uc).astype(u_ref.dtype)
    for k in range(SC_W - 1):
        r = seq + SUBLANES * k + SUBLANES - 1
        sn_ref[0, k:k + 1, :] = uh_ref[r:r + 1, :]


def _mix_b_sample_kernel(sb_ref, sc_ref, sx_ref, buf_ref, w_ref, u_ref, sn_ref, uh_ref,
                         *, nseq, seq):
    halo = (SC_W - 1) * nseq
    m = nseq * seq
    uh_ref[0:halo, :] = buf_ref[...]
    uh_ref[halo:halo + m, :] = sc_ref[...] * sx_ref[...]
    taps = [uh_ref[nseq * k:nseq * k + m, :] for k in range(SC_W)]
    uc = _conv_taps(taps, w_ref[...], SC_W)
    u_ref[...] = (sb_ref[...] * uc).astype(u_ref.dtype)
    sn_ref[...] = uh_ref[m:m + halo, :]


def _mix_prompt_kernel(x_ref, g_ref, w4_ref, b4_ref, wr_ref, wi_ref, br_ref, bi_ref, lam_ref,
                       sb_ref, sc_ref, sx_ref, wsc_ref,
                       ua_ref, hn_ref, cn_ref, ub_ref, sn_ref,
                       xh_ref, hh_ref, pp_ref, uh_ref, *, seq, rc):
    _mix_b_prompt_kernel(sb_ref, sc_ref, sx_ref, wsc_ref, ub_ref, sn_ref, uh_ref,
                         seq=seq, rc=rc)
    _mix_a_prompt_kernel(x_ref, g_ref, w4_ref, b4_ref, wr_ref, wi_ref, br_ref, bi_ref,
                         lam_ref, ua_ref, hn_ref, cn_ref, xh_ref, hh_ref, pp_ref,
                         seq=seq, rc=rc)


def _mix_prompt(z, offs, l, w4, b4, wr, wi, br, bi, lam, wsc, batch, seq, tc=256, rc=256):
    da, db = w4.shape[1], wsc.shape[1]
    tc, rc = min(tc, da), min(rc, seq)
    nj = da // tc
    tb = db // nj
    vec = pl.BlockSpec((1, tc), lambda j, b: (0, j))
    blk = pl.BlockSpec((None, tc // RG_BLK, RG_BLK, RG_BLK), lambda j, b: (l, j, 0, 0))
    vm = (2 * (2 * seq * tc * 4 + seq * tc * 2 + 2 * tc * RG_BLK * 4 + 3 * seq * tb * 4
               + seq * tb * 2) + 3 * seq * tc * 4 + (seq + 16) * tb * 4)

    def cola(off):
        return pl.BlockSpec((seq, tc), lambda j, b: (b, off // tc + j))

    def colb(off):
        return pl.BlockSpec((seq, tb), lambda j, b: (b, off // tb + j))

    return pl.pallas_call(
        functools.partial(_mix_prompt_kernel, seq=seq, rc=rc),
        grid=(nj, batch),
        in_specs=[cola(offs["x"]), cola(offs["g"]),
                  pl.BlockSpec((CONV4_W, tc), lambda j, b: (0, j)),
                  vec, blk, blk, vec, vec, vec,
                  colb(offs["sb"]), colb(offs["sc"]), colb(offs["sx"]),
                  pl.BlockSpec((SC_W, tb), lambda j, b: (0, j))],
        out_specs=[pl.BlockSpec((seq, tc), lambda j, b: (b, j)),
                   pl.BlockSpec((1, 1, tc), lambda j, b: (b, 0, j)),
                   pl.BlockSpec((1, CONV4_W - 1, tc), lambda j, b: (b, 0, j)),
                   pl.BlockSpec((seq, tb), lambda j, b: (b, j)),
                   pl.BlockSpec((1, SC_W - 1, tb), lambda j, b: (b, 0, j))],
        out_shape=[jax.ShapeDtypeStruct((batch * seq, da), BF16),
                   jax.ShapeDtypeStruct((batch, 1, da), F32),
                   jax.ShapeDtypeStruct((batch, CONV4_W - 1, da), F32),
                   jax.ShapeDtypeStruct((batch * seq, db), BF16),
                   jax.ShapeDtypeStruct((batch, SC_W - 1, db), F32)],
        scratch_shapes=[pltpu.VMEM(((CONV4_W - 1) * SUBLANES + rc, tc), F32),
                        pltpu.VMEM((seq, tc), F32),
                        pltpu.VMEM((seq, tc), F32),
                        pltpu.VMEM(((SC_W - 1) * SUBLANES + seq, tb), F32)],
        compiler_params=_params(("parallel", "arbitrary"), vm),
        name="mix_prompt",
    )(z, z, w4, b4, wr, wi, br, bi, lam, z, z, z, wsc)


def _mix_sample_kernel(x_ref, g_ref, h0_ref, c4_ref, w4_ref, b4_ref, wr_ref, wi_ref, br_ref,
                       bi_ref, lam_ref, sb_ref, sc_ref, sx_ref, buf_ref, wsc_ref,
                       ua_ref, hn_ref, cn_ref, ub_ref, sn_ref, xh_ref, uh_ref, *, nseq, seq):
    _mix_b_sample_kernel(sb_ref, sc_ref, sx_ref, buf_ref, wsc_ref, ub_ref, sn_ref, uh_ref,
                         nseq=nseq, seq=seq)
    _mix_a_sample_kernel(x_ref, g_ref, h0_ref, c4_ref, w4_ref, b4_ref, wr_ref, wi_ref,
                         br_ref, bi_ref, lam_ref, ua_ref, hn_ref, cn_ref, xh_ref,
                         nseq=nseq, seq=seq)


def _mix_sample(z, offs, h0, c4, buf, l, w4, b4, wr, wi, br, bi, lam, wsc, nseq, seq, tc=256):
    da, db = w4.shape[1], wsc.shape[1]
    tc = min(tc, da)
    nj = da // tc
    tb = db // nj
    m = nseq * seq
    ha, hb = (CONV4_W - 1) * nseq, (SC_W - 1) * nseq
    vec = pl.BlockSpec((1, tc), lambda j: (0, j))
    blk = pl.BlockSpec((None, tc // RG_BLK, RG_BLK, RG_BLK), lambda j: (l, j, 0, 0))
    vm = (2 * (2 * m * tc * 4 + m * tc * 2 + nseq * tc * 8 + 2 * ha * tc * 4
               + 2 * tc * RG_BLK * 4 + 3 * m * tb * 4 + m * tb * 2 + 2 * hb * tb * 4)
          + 8 * m * tc * 4 + 4 * m * tb * 4)

    def cola(off):
        return pl.BlockSpec((m, tc), lambda j: (0, off // tc + j))

    def colb(off):
        return pl.BlockSpec((m, tb), lambda j: (0, off // tb + j))

    return pl.pallas_call(
        functools.partial(_mix_sample_kernel, nseq=nseq, seq=seq),
        grid=(nj,),
        in_specs=[cola(offs["x"]), cola(offs["g"]),
                  pl.BlockSpec((nseq, tc), lambda j: (0, j)),
                  pl.BlockSpec((ha, tc), lambda j: (0, j)),
                  pl.BlockSpec((CONV4_W, tc), lambda j: (0, j)),
                  vec, blk, blk, vec, vec, vec,
                  colb(offs["sb"]), colb(offs["sc"]), colb(offs["sx"]),
                  pl.BlockSpec((hb, tb), lambda j: (0, j)),
                  pl.BlockSpec((SC_W, tb), lambda j: (0, j))],
        out_specs=[pl.BlockSpec((m, tc), lambda j: (0, j)),
                   pl.BlockSpec((nseq, tc), lambda j: (0, j)),
                   pl.BlockSpec((ha, tc), lambda j: (0, j)),
                   pl.BlockSpec((m, tb), lambda j: (0, j)),
                   pl.BlockSpec((hb, tb), lambda j: (0, j))],
        out_shape=[jax.ShapeDtypeStruct((m, da), BF16),
                   jax.ShapeDtypeStruct((nseq, da), F32),
                   jax.ShapeDtypeStruct((ha, da), F32),
                   jax.ShapeDtypeStruct((m, db), BF16),
                   jax.ShapeDtypeStruct((hb, db), F32)],
        scratch_shapes=[pltpu.VMEM((ha + m, tc), F32),
                        pltpu.VMEM((hb + m, tb), F32)],
        compiler_params=_params(("parallel",), vm),
        name="mix_sample",
    )(z, z, h0, c4, w4, b4, wr, wi, br, bi, lam, z, z, z, buf, wsc)


def _time_major(s):
    b, w, c = s.shape
    return jnp.swapaxes(s, 0, 1).reshape(w * b, c)


def _batch_major(s, b):
    return jnp.swapaxes(s.reshape(-1, b, s.shape[1]), 0, 1)


def _layer(x, h, l, lw, gnext, group):
    d_rnn = lw["w_conv4"].shape[1]
    d_conv = lw["w_sc"].shape[1]
    off_x, off_g = 0, d_rnn
    off_sb = 2 * d_rnn
    off_sc = off_sb + d_conv
    off_sx = off_sc + d_conv
    off_ga = off_sx + d_conv
    off_gb = off_ga + x.shape[1]
    nb, seq = group["batch"], group["seq"]

    offs = {"x": off_x, "g": off_g, "sb": off_sb, "sc": off_sc, "sx": off_sx}
    z = _mm(h, lw["w_in"], l, F32, False, "mm_in")
    mixw = (l, lw["w_conv4"], lw["b_conv4"], lw["w_rg_r"], lw["w_rg_i"], lw["b_rg_r"],
            lw["b_rg_i"], lw["rg_lambda"], lw["w_sc"])
    if group["kind"] == "prompt":
        ua, hn, cn, ub, sn = _mix_prompt(z, offs, *mixw, nb, seq)
        hn = hn.reshape(nb, d_rnn)
    else:
        ua, hn, cn, ub, sn = _mix_sample(z, offs, group["h0"], _time_major(group["c4"]),
                                         _time_major(group["sc"]), *mixw, nb, seq)
        cn = _batch_major(cn, nb)
        sn = _batch_major(sn, nb)
    pp = group["kind"] == "prompt"
    mm_norm = _mm_norm_pp if pp else _mm_norm
    m = _merge(ua, ub, lw["w_out_a"], lw["w_out_b"], l, z, off_ga, off_gb)
    x1, hm = mm_norm(m, lw["w_o"], l, x, lw["norm_mix_post"], lw["norm_mlp_pre"], "mm_o")
    a = _mm(hm, lw["w_mlp_up"], l, BF16, True, "mm_up", tiled_out=True)
    x2, hnext = mm_norm(a, lw["w_mlp_down"], l, x1, lw["norm_mlp_post"], gnext, "mm_down")
    return x2, hnext, hn, cn, sn


def kernel(x_prompt, x_sample, state_rglru_h, state_conv4, state_shortconv, norm_mix_pre,
           norm_mix_post, w_in, w_conv4, b_conv4, w_rg_r, b_rg_r, w_rg_i, b_rg_i, rg_lambda,
           w_out_a, w_sc, w_out_b, w_o, norm_mlp_pre, norm_mlp_post, w_mlp_up, w_mlp_down):
    depth = w_in.shape[0]
    bp, sp, d = x_prompt.shape
    bs, ss, _ = x_sample.shape
    assert sp % (SUBLANES * SUBLANES) == 0 and bs % SUBLANES == 0

    xp = jnp.swapaxes(x_prompt.reshape(bp, SUBLANES, sp // SUBLANES, d), 1, 2)
    xp = xp.reshape(bp * sp, d)
    xs = jnp.swapaxes(x_sample, 0, 1).reshape(ss * bs, d)
    hp = _norm(xp, norm_mix_pre[0].reshape(1, d))
    hs = _norm(xs, norm_mix_pre[0].reshape(1, d))

    outs_p, outs_s = [], []
    for l in range(depth):
        lw = {
            "w_in": w_in,
            "w_conv4": w_conv4[l],
            "b_conv4": b_conv4[l].reshape(1, -1),
            "w_rg_r": w_rg_r,
            "w_rg_i": w_rg_i,
            "b_rg_r": b_rg_r[l].reshape(1, -1),
            "b_rg_i": b_rg_i[l].reshape(1, -1),
            "rg_lambda": rg_lambda[l].reshape(1, -1),
            "w_out_a": w_out_a,
            "w_sc": w_sc[l],
            "w_out_b": w_out_b,
            "w_o": w_o,
            "norm_mix_post": norm_mix_post[l].reshape(1, -1),
            "norm_mlp_pre": norm_mlp_pre[l].reshape(1, -1),
            "norm_mlp_post": norm_mlp_post[l].reshape(1, -1),
            "w_mlp_up": w_mlp_up,
            "w_mlp_down": w_mlp_down,
        }
        gnext = norm_mix_pre[(l + 1) % depth].reshape(1, -1)
        gp = {"kind": "prompt", "batch": bp, "seq": sp}
        gs = {"kind": "sample", "batch": bs, "seq": ss, "h0": state_rglru_h[l],
              "c4": state_conv4[l], "sc": state_shortconv[l]}
        xp, hp, hnp, cnp, snp = _layer(xp, hp, l, lw, gnext, gp)
        xs, hs, hns, cns, sns = _layer(xs, hs, l, lw, gnext, gs)
        outs_p.append((hnp, cnp, snp))
        outs_s.append((hns, cns, sns))

    yp = jnp.swapaxes(xp.reshape(bp, sp // SUBLANES, SUBLANES, d), 1, 2).reshape(bp, sp, d)
    ys = jnp.swapaxes(xs.reshape(ss, bs, d), 0, 1)
    return (yp, ys,
            jnp.stack([o[0] for o in outs_p]), jnp.stack([o[1] for o in outs_p]),
            jnp.stack([o[2] for o in outs_p]),
            jnp.stack([o[0] for o in outs_s]), jnp.stack([o[1] for o in outs_s]),
            jnp.stack([o[2] for o in outs_s]))
```

```python
import functools

import jax
import jax.numpy as jnp
from jax import lax
from jax.experimental import pallas as pl
from jax.experimental.pallas import tpu as pltpu

RG_BLK = 256
RG_C = 8.0
CONV4_W = 4
SC_W = 3
EPS = 1e-6

SUBLANES = 8
V7X_VMEM_BUDGET = 58 * 1024 * 1024

F32 = jnp.float32
BF16 = jnp.bfloat16


def _params(semantics, vmem_bytes):
    return pltpu.CompilerParams(
        dimension_semantics=semantics,
        vmem_limit_bytes=min(int(vmem_bytes) + (12 << 20), V7X_VMEM_BUDGET),
    )


def _rms(x, g):
    ms = jnp.mean(x * x, axis=-1, keepdims=True)
    return (x * lax.rsqrt(ms + EPS)) * g


def _sigmoid(x):
    return 0.5 * jnp.tanh(0.5 * x) + 0.5


GELU_C = 0.7978845608028654
GELU_C3 = GELU_C * 0.044715


def _gelu(x):
    hx = 0.5 * x
    return hx * jnp.tanh(x * (GELU_C + GELU_C3 * (x * x))) + hx


def _norm_kernel(x_ref, g_ref, o_ref):
    o_ref[...] = _rms(x_ref[...], g_ref[...]).astype(o_ref.dtype)


def _norm(x, g, tm=256):
    m, d = x.shape
    tm = min(tm, m)
    return pl.pallas_call(
        _norm_kernel,
        grid=(m // tm,),
        in_specs=[pl.BlockSpec((tm, d), lambda i: (i, 0)),
                  pl.BlockSpec((1, d), lambda i: (0, 0))],
        out_specs=pl.BlockSpec((tm, d), lambda i: (i, 0)),
        out_shape=jax.ShapeDtypeStruct((m, d), BF16),
        compiler_params=_params(("parallel",), 2 * tm * d * 6),
        name="norm0",
    )(x, g)


def _mm_kernel(x_ref, w_ref, o_ref, *, relu2):
    acc = jnp.dot(x_ref[...], w_ref[...].astype(BF16), preferred_element_type=F32)
    if relu2:
        acc = jnp.square(jnp.maximum(acc, 0.0))
    o_ref[...] = acc.astype(o_ref.dtype)


def _mm(x, w, l, out_dtype, relu2, name, tm=2048, tn=512, tiled_out=False):
    m, k = x.shape
    n = w.shape[2]
    tm, tn = min(tm, m), min(tn, n)
    if tiled_out:
        out_spec = pl.BlockSpec((None, tm, tn), lambda i, j: (j, i, 0))
        out_shape = jax.ShapeDtypeStruct((n // tn, m, tn), out_dtype)
    else:
        out_spec = pl.BlockSpec((tm, tn), lambda i, j: (i, j))
        out_shape = jax.ShapeDtypeStruct((m, n), out_dtype)
    vm = (tm * k * 2 + 2 * (k * tn * 4 + tm * tn * jnp.dtype(out_dtype).itemsize)
          + k * tn * 2 + tm * tn * 4)
    return pl.pallas_call(
        functools.partial(_mm_kernel, relu2=relu2),
        grid=(m // tm, n // tn),
        in_specs=[pl.BlockSpec((tm, k), lambda i, j: (i, 0), pipeline_mode=pl.Buffered(1)),
                  pl.BlockSpec((None, k, tn), lambda i, j: (l, 0, j))],
        out_specs=out_spec,
        out_shape=out_shape,
        compiler_params=_params(("parallel", "arbitrary"), vm),
        name=name,
    )(x, w)


def _merge_kernel(ua_ref, ub_ref, wa_ref, wb_ref, ga_ref, gb_ref, *rest):
    o_ref = rest[-1] if len(rest) == 1 else rest[1]
    ya = jnp.dot(ua_ref[...], wa_ref[...].astype(BF16), preferred_element_type=F32)
    yb = jnp.dot(ub_ref[...], wb_ref[...].astype(BF16), preferred_element_type=F32)
    m = _sigmoid(ga_ref[...]) * ya + _sigmoid(gb_ref[...]) * yb
    o_ref[...] = m.astype(o_ref.dtype)
    if len(rest) == 3:
        wo_ref, _, wob_ref = rest
        wob_ref[...] = wo_ref[...].astype(wob_ref.dtype)


def _merge(ua, ub, wa, wb, l, z, off_ga, off_gb, wo=None, tm=1024, tn=256):
    m, ka = ua.shape
    kb = ub.shape[1]
    n = wa.shape[2]
    tm, tn = min(tm, m), min(tn, n)
    ni, nj = m // tm, n // tn
    vm = (2 * (tm * ka * 2 + tm * kb * 2 + ka * tn * 4 + kb * tn * 4
               + 2 * tm * tn * 4 + tm * tn * 2)
          + (ka + kb) * tn * 2 + 2 * tm * tn * 4)
    in_specs = [pl.BlockSpec((tm, ka), lambda i, j: (i, 0)),
                pl.BlockSpec((tm, kb), lambda i, j: (i, 0)),
                pl.BlockSpec((None, ka, tn), lambda i, j: (l, 0, j)),
                pl.BlockSpec((None, kb, tn), lambda i, j: (l, 0, j)),
                pl.BlockSpec((tm, tn), lambda i, j: (i, off_ga // tn + j)),
                pl.BlockSpec((tm, tn), lambda i, j: (i, off_gb // tn + j))]
    out_specs = [pl.BlockSpec((None, tm, tn), lambda i, j: (j, i, 0))]
    out_shape = [jax.ShapeDtypeStruct((nj, m, tn), BF16)]
    args = [ua, ub, wa, wb, z, z]
    if wo is not None:
        kw, nw = wo.shape[1], wo.shape[2]
        rows = kw // (ni * nj)
        assert rows % 16 == 0
        in_specs.append(pl.BlockSpec((None, rows, nw), lambda i, j: (l, i * nj + j, 0)))
        out_specs.append(pl.BlockSpec((None, rows, nw), lambda i, j: (0, i * nj + j, 0)))
        out_shape.append(jax.ShapeDtypeStruct((1, kw, nw), BF16))
        args.append(wo)
        vm += 2 * rows * nw * 6
    return pl.pallas_call(
        _merge_kernel,
        grid=(ni, nj),
        in_specs=in_specs,
        out_specs=out_specs,
        out_shape=out_shape,
        compiler_params=_params(("arbitrary", "arbitrary"), vm),
        name="merge",
    )(*args)


def _tiles_dot(a_ref, w_ref):
    ksub = a_ref.shape[0]
    a = a_ref[0] if ksub == 1 else jnp.concatenate([a_ref[s] for s in range(ksub)], axis=1)
    return jnp.dot(a, w_ref[...].astype(BF16), preferred_element_type=F32)


def _mm_norm_kernel(a_ref, w_ref, x_ref, gpost_ref, gnext_ref, xo_ref, ho_ref, acc_ref,
                    *, nk, te):
    k = pl.program_id(1)

    @pl.when(k == 0)
    def _():
        acc_ref[...] = _tiles_dot(a_ref, w_ref)

    @pl.when(jnp.logical_and(k > 0, k < nk))
    def _():
        acc_ref[...] += _tiles_dot(a_ref, w_ref)

    @pl.when(k >= nk)
    def _():
        r0 = pl.multiple_of((k - nk) * te, te)
        o = acc_ref[pl.ds(r0, te), :]
        x1 = x_ref[...] + _rms(o, gpost_ref[...])
        xo_ref[...] = x1
        ho_ref[...] = _rms(x1, gnext_ref[...]).astype(ho_ref.dtype)


def _mm_norm(a, w, l, x, gpost, gnext, name, tm=1024, te=128, ksub=1):
    nka, m, tka = a.shape
    nk, tk = nka // ksub, ksub * tka
    d = w.shape[2]
    tm = min(tm, m)
    te = min(te, tm)
    ne = tm // te
    vm = (2 * (tm * tk * 2 + tk * d * 4 + te * d * 4 + te * d * 4 + te * d * 2)
          + tm * d * 4 + tk * d * 2)

    def slab(i, k):
        return (i * ne + jnp.maximum(k - nk, 0), 0)

    return pl.pallas_call(
        functools.partial(_mm_norm_kernel, nk=nk, te=te),
        grid=(m // tm, nk + ne),
        in_specs=[pl.BlockSpec((ksub, tm, tka), lambda i, k: (jnp.minimum(k, nk - 1), i, 0)),
                  pl.BlockSpec((None, tk, d), lambda i, k: (l, jnp.minimum(k, nk - 1), 0)),
                  pl.BlockSpec((te, d), slab),
                  pl.BlockSpec((1, d), lambda i, k: (0, 0)),
                  pl.BlockSpec((1, d), lambda i, k: (0, 0))],
        out_specs=[pl.BlockSpec((te, d), slab),
                   pl.BlockSpec((te, d), slab)],
        out_shape=[jax.ShapeDtypeStruct((m, d), F32),
                   jax.ShapeDtypeStruct((m, d), BF16)],
        scratch_shapes=[pltpu.VMEM((tm, d), F32)],
        compiler_params=_params(("parallel", "arbitrary"), vm),
        name=name,
    )(a, w, x, gpost, gnext)


def _mm_norm_pp_kernel(a_ref, w_ref, x_ref, gpost_ref, gnext_ref, xo_ref, ho_ref,
                       acc0_ref, acc1_ref, *, ni, te):
    p = pl.program_id(0)
    k = pl.program_id(1)
    accs = (acc0_ref, acc1_ref)

    def contract(acc_ref):
        acc_ref[...] += _tiles_dot(a_ref, w_ref)

    def drain(acc_ref):
        r0 = pl.multiple_of(k * te, te)
        o = acc_ref[pl.ds(r0, te), :]
        acc_ref[pl.ds(r0, te), :] = jnp.zeros_like(o)
        x1 = x_ref[...] + _rms(o, gpost_ref[...])
        xo_ref[...] = x1
        ho_ref[...] = _rms(x1, gnext_ref[...]).astype(ho_ref.dtype)

    @pl.when(p == 0)
    def _():
        @pl.when(k == 0)
        def _():
            acc0_ref[...] = jnp.zeros(acc0_ref.shape, F32)
            acc1_ref[...] = jnp.zeros(acc1_ref.shape, F32)
        contract(acc0_ref)

    for parity in (0, 1):
        @pl.when(jnp.logical_and(jnp.logical_and(p > 0, p < ni), p % 2 == parity))
        def _():
            contract(accs[parity])
            drain(accs[1 - parity])

    @pl.when(p == ni)
    def _():
        drain(accs[(ni - 1) % 2])


def _mm_norm_pp(a, w, l, x, gpost, gnext, name, tm=1024, ksub=1):
    nka, m, tka = a.shape
    nk, tk = nka // ksub, ksub * tka
    d = w.shape[2]
    tm = min(tm, m // 2)
    ni = m // tm
    te = tm // nk
    assert te % 16 == 0
    vm = (2 * (tm * tk * 2 + tk * d * 4 + te * d * 4 + te * d * 4 + te * d * 2)
          + 2 * tm * d * 4 + tk * d * 2)

    def kk(p, k):
        return jnp.where(p < ni, k, nk - 1)

    def slab(p, k):
        return (jnp.where(p > 0, (p - 1) * nk + k, 0), 0)

    return pl.pallas_call(
        functools.partial(_mm_norm_pp_kernel, ni=ni, te=te),
        grid=(ni + 1, nk),
        in_specs=[pl.BlockSpec((ksub, tm, tka),
                               lambda p, k: (kk(p, k), jnp.minimum(p, ni - 1), 0)),
                  pl.BlockSpec((None, tk, d), lambda p, k: (l, kk(p, k), 0)),
                  pl.BlockSpec((te, d), slab),
                  pl.BlockSpec((1, d), lambda p, k: (0, 0)),
                  pl.BlockSpec((1, d), lambda p, k: (0, 0))],
        out_specs=[pl.BlockSpec((te, d), slab),
                   pl.BlockSpec((te, d), slab)],
        out_shape=[jax.ShapeDtypeStruct((m, d), F32),
                   jax.ShapeDtypeStruct((m, d), BF16)],
        scratch_shapes=[pltpu.VMEM((tm, d), F32), pltpu.VMEM((tm, d), F32)],
        compiler_params=_params(("arbitrary", "arbitrary"), vm),
        name=name,
    )(a, w, x, gpost, gnext)


def _softplus_neg(lam):
    return jnp.maximum(-lam, 0.0) + jnp.log1p(jnp.exp(-jnp.abs(lam)))


def _rglru_coeffs(xc, wr_ref, wi_ref, br, bi, sp):
    xcb = xc.astype(BF16)
    hbr, hbi = 0.5 * br, 0.5 * bi
    c = (-0.5 * RG_C) * sp
    a_parts, b_parts = [], []
    for g in range(xc.shape[1] // RG_BLK):
        sl = slice(g * RG_BLK, (g + 1) * RG_BLK)
        xg = xcb[:, sl]
        tr = jnp.tanh(jnp.dot(xg, (0.5 * wr_ref[g]).astype(BF16),
                              preferred_element_type=F32) + hbr[:, sl])
        ti = jnp.tanh(jnp.dot(xg, (0.5 * wi_ref[g]).astype(BF16),
                              preferred_element_type=F32) + hbi[:, sl])
        log_a = c[:, sl] * tr + c[:, sl]
        a = jnp.exp(log_a)
        y = jnp.tanh(log_a) * (-1.0 - a * a)
        mult = jnp.where(y > 0.0, y * lax.rsqrt(y), 0.0)
        a_parts.append(a)
        b_parts.append((mult * xc[:, sl]) * (0.5 * ti + 0.5))
    if len(a_parts) == 1:
        return a_parts[0], b_parts[0]
    return jnp.concatenate(a_parts, axis=1), jnp.concatenate(b_parts, axis=1)


def _conv_taps(taps, w, width):
    acc = taps[0] * w[0:1, :]
    for k in range(1, width):
        acc = acc + taps[k] * w[k:k + 1, :]
    return acc


def _scan8(a, b):
    row = lax.broadcasted_iota(jnp.int32, a.shape, 0)
    for s in (1, 2, 4):
        keep = row >= s
        b = jnp.where(keep, a * pltpu.roll(b, s, 0) + b, b)
        a = jnp.where(keep, a * pltpu.roll(a, s, 0), a)
    return a, b


def _prev_segment(grp):
    row = lax.broadcasted_iota(jnp.int32, grp.shape, 0)
    return jnp.where(row == 0, 0.0, pltpu.roll(grp, 1, 0))


def _mix_a_prompt_kernel(x_ref, g_ref, w4_ref, b4_ref, wr_ref, wi_ref, br_ref, bi_ref,
                         lam_ref, u_ref, hn_ref, cn_ref, xh_ref, hh_ref, pp_ref,
                         *, seq, rc):
    tc = x_ref.shape[1]
    halo = (CONV4_W - 1) * SUBLANES
    w4 = w4_ref[...]
    b4 = b4_ref[...]
    br = br_ref[...]
    bi = bi_ref[...]
    sp = _softplus_neg(lam_ref[...])

    for k in range(CONV4_W - 1):
        grp = x_ref[seq - halo + SUBLANES * k:seq - halo + SUBLANES * (k + 1), :]
        xh_ref[SUBLANES * k:SUBLANES * (k + 1), :] = _prev_segment(grp)

    h = jnp.zeros((SUBLANES, tc), F32)
    p = jnp.ones((SUBLANES, tc), F32)
    for c in range(seq // rc):
        r0 = c * rc
        if c == 0:
            xh_ref[halo:halo + rc, :] = x_ref[0:rc, :]
            taps = [xh_ref[SUBLANES * k:SUBLANES * k + rc, :] for k in range(CONV4_W)]
        else:
            taps = [x_ref[r0 - halo + SUBLANES * k:r0 - halo + SUBLANES * k + rc, :]
                    for k in range(CONV4_W)]
        xc = _conv_taps(taps, w4, CONV4_W) + b4
        a, b = _rglru_coeffs(xc, wr_ref, wi_ref, br, bi, sp)
        for q in range(rc // SUBLANES):
            aq = a[SUBLANES * q:SUBLANES * (q + 1), :]
            h = aq * h + b[SUBLANES * q:SUBLANES * (q + 1), :]
            p = aq * p
            hh_ref[r0 + SUBLANES * q:r0 + SUBLANES * (q + 1), :] = h
            pp_ref[r0 + SUBLANES * q:r0 + SUBLANES * (q + 1), :] = p

    _, hc = _scan8(p, h)
    cin = _prev_segment(hc)
    hn_ref[0] = hc[SUBLANES - 1:SUBLANES, :]

    for c in range(seq // rc):
        r0 = c * rc
        cin_t = jnp.tile(cin, (rc // SUBLANES, 1))
        hfull = hh_ref[r0:r0 + rc, :] + pp_ref[r0:r0 + rc, :] * cin_t
        u_ref[r0:r0 + rc, :] = (_gelu(g_ref[r0:r0 + rc, :]) * hfull).astype(u_ref.dtype)

    for k in range(CONV4_W - 1):
        r = seq - halo + SUBLANES * k + SUBLANES - 1
        cn_ref[0, k:k + 1, :] = x_ref[r:r + 1, :]


def _mix_a_sample_kernel(x_ref, g_ref, h0_ref, c4_ref, w4_ref, b4_ref, wr_ref, wi_ref,
                         br_ref, bi_ref, lam_ref, u_ref, hn_ref, cn_ref, xh_ref,
                         *, nseq, seq):
    halo = (CONV4_W - 1) * nseq
    m = nseq * seq
    sp = _softplus_neg(lam_ref[...])
    xh_ref[0:halo, :] = c4_ref[...]
    xh_ref[halo:halo + m, :] = x_ref[...]
    taps = [xh_ref[nseq * k:nseq * k + m, :] for k in range(CONV4_W)]
    xc = _conv_taps(taps, w4_ref[...], CONV4_W) + b4_ref[...]
    a, b = _rglru_coeffs(xc, wr_ref, wi_ref, br_ref[...], bi_ref[...], sp)
    h = h0_ref[...]
    for t in range(seq):
        rows = slice(nseq * t, nseq * (t + 1))
        h = a[rows, :] * h + b[rows, :]
        u_ref[rows, :] = (_gelu(g_ref[rows, :]) * h).astype(u_ref.dtype)
    hn_ref[...] = h
    cn_ref[...] = xh_ref[m:m + halo, :]


def _mix_b_prompt_kernel(sb_ref, sc_ref, sx_ref, w_ref, u_ref, sn_ref, uh_ref, *, seq, rc):
    halo = (SC_W - 1) * SUBLANES
    w = w_ref[...]
    for c in range(seq // rc):
        r0 = c * rc
        uh_ref[halo + r0:halo + r0 + rc, :] = sc_ref[r0:r0 + rc, :] * sx_ref[r0:r0 + rc, :]
    for k in range(SC_W - 1):
        grp = uh_ref[seq + SUBLANES * k:seq + SUBLANES * (k + 1), :]
        uh_ref[SUBLANES * k:SUBLANES * (k + 1), :] = _prev_segment(grp)
    for c in range(seq // rc):
        r0 = c * rc
        taps = [uh_ref[r0 + SUBLANES * k:r0 + SUBLANES * k + rc, :] for k in range(SC_W)]
        uc = _conv_taps(taps, w, SC_W)
        u_ref[r0:r0 + rc, :] = (sb_ref[r0:r0 + rc, :] * uc).astype(u_ref.dtype)
    for k in range(SC_W - 1):
        r = seq + SUBLANES * k + SUBLANES - 1
        sn_ref[0, k:k + 1, :] = uh_ref[r:r + 1, :]


def _mix_b_sample_kernel(sb_ref, sc_ref, sx_ref, buf_ref, w_ref, u_ref, sn_ref, uh_ref,
                         *, nseq, seq):
    halo = (SC_W - 1) * nseq
    m = nseq * seq
    uh_ref[0:halo, :] = buf_ref[...]
    uh_ref[halo:halo + m, :] = sc_ref[...] * sx_ref[...]
    taps = [uh_ref[nseq * k:nseq * k + m, :] for k in range(SC_W)]
    uc = _conv_taps(taps, w_ref[...], SC_W)
    u_ref[...] = (sb_ref[...] * uc).astype(u_ref.dtype)
    sn_ref[...] = uh_ref[m:m + halo, :]


def _mix_prompt_kernel(x_ref, g_ref, w4_ref, b4_ref, wr_ref, wi_ref, br_ref, bi_ref, lam_ref,
                       sb_ref, sc_ref, sx_ref, wsc_ref,
                       ua_ref, hn_ref, cn_ref, ub_ref, sn_ref,
                       xh_ref, hh_ref, pp_ref, uh_ref, *, seq, rc):
    _mix_b_prompt_kernel(sb_ref, sc_ref, sx_ref, wsc_ref, ub_ref, sn_ref, uh_ref,
                         seq=seq, rc=rc)
    _mix_a_prompt_kernel(x_ref, g_ref, w4_ref, b4_ref, wr_ref, wi_ref, br_ref, bi_ref,
                         lam_ref, ua_ref, hn_ref, cn_ref, xh_ref, hh_ref, pp_ref,
                         seq=seq, rc=rc)


def _mix_prompt(z, offs, l, w4, b4, wr, wi, br, bi, lam, wsc, batch, seq, tc=256, rc=256):
    da, db = w4.shape[1], wsc.shape[1]
    tc, rc = min(tc, da), min(rc, seq)
    nj = da // tc
    tb = db // nj
    vec = pl.BlockSpec((1, tc), lambda j, b: (0, j))
    blk = pl.BlockSpec((None, tc // RG_BLK, RG_BLK, RG_BLK), lambda j, b: (l, j, 0, 0))
    vm = (2 * (2 * seq * tc * 4 + seq * tc * 2 + 2 * tc * RG_BLK * 4 + 3 * seq * tb * 4
               + seq * tb * 2) + 3 * seq * tc * 4 + (seq + 16) * tb * 4)

    def cola(off):
        return pl.BlockSpec((seq, tc), lambda j, b: (b, off // tc + j))

    def colb(off):
        return pl.BlockSpec((seq, tb), lambda j, b: (b, off // tb + j))

    return pl.pallas_call(
        functools.partial(_mix_prompt_kernel, seq=seq, rc=rc),
        grid=(nj, batch),
        in_specs=[cola(offs["x"]), cola(offs["g"]),
                  pl.BlockSpec((CONV4_W, tc), lambda j, b: (0, j)),
                  vec, blk, blk, vec, vec, vec,
                  colb(offs["sb"]), colb(offs["sc"]), colb(offs["sx"]),
                  pl.BlockSpec((SC_W, tb), lambda j, b: (0, j))],
        out_specs=[pl.BlockSpec((seq, tc), lambda j, b: (b, j)),
                   pl.BlockSpec((1, 1, tc), lambda j, b: (b, 0, j)),
                   pl.BlockSpec((1, CONV4_W - 1, tc), lambda j, b: (b, 0, j)),
                   pl.BlockSpec((seq, tb), lambda j, b: (b, j)),
                   pl.BlockSpec((1, SC_W - 1, tb), lambda j, b: (b, 0, j))],
        out_shape=[jax.ShapeDtypeStruct((batch * seq, da), BF16),
                   jax.ShapeDtypeStruct((batch, 1, da), F32),
                   jax.ShapeDtypeStruct((batch, CONV4_W - 1, da), F32),
                   jax.ShapeDtypeStruct((batch * seq, db), BF16),
                   jax.ShapeDtypeStruct((batch, SC_W - 1, db), F32)],
        scratch_shapes=[pltpu.VMEM(((CONV4_W - 1) * SUBLANES + rc, tc), F32),
                        pltpu.VMEM((seq, tc), F32),
                        pltpu.VMEM((seq, tc), F32),
                        pltpu.VMEM(((SC_W - 1) * SUBLANES + seq, tb), F32)],
        compiler_params=_params(("parallel", "arbitrary"), vm),
        name="mix_prompt",
    )(z, z, w4, b4, wr, wi, br, bi, lam, z, z, z, wsc)


def _mix_sample_kernel(x_ref, g_ref, h0_ref, c4_ref, w4_ref, b4_ref, wr_ref, wi_ref, br_ref,
                       bi_ref, lam_ref, sb_ref, sc_ref, sx_ref, buf_ref, wsc_ref,
                       ua_ref, hn_ref, cn_ref, ub_ref, sn_ref, xh_ref, uh_ref, *, nseq, seq):
    _mix_b_sample_kernel(sb_ref, sc_ref, sx_ref, buf_ref, wsc_ref, ub_ref, sn_ref, uh_ref,
                         nseq=nseq, seq=seq)
    _mix_a_sample_kernel(x_ref, g_ref, h0_ref, c4_ref, w4_ref, b4_ref, wr_ref, wi_ref,
                         br_ref, bi_ref, lam_ref, ua_ref, hn_ref, cn_ref, xh_ref,
                         nseq=nseq, seq=seq)


def _mix_sample(z, offs, h0, c4, buf, l, w4, b4, wr, wi, br, bi, lam, wsc, nseq, seq, tc=256):
    da, db = w4.shape[1], wsc.shape[1]
    tc = min(tc, da)
    nj = da // tc
    tb = db // nj
    m = nseq * seq
    ha, hb = (CONV4_W - 1) * nseq, (SC_W - 1) * nseq
    vec = pl.BlockSpec((1, tc), lambda j: (0, j))
    blk = pl.BlockSpec((None, tc // RG_BLK, RG_BLK, RG_BLK), lambda j: (l, j, 0, 0))
    vm = (2 * (2 * m * tc * 4 + m * tc * 2 + nseq * tc * 8 + 2 * ha * tc * 4
               + 2 * tc * RG_BLK * 4 + 3 * m * tb * 4 + m * tb * 2 + 2 * hb * tb * 4)
          + 8 * m * tc * 4 + 4 * m * tb * 4)

    def cola(off):
        return pl.BlockSpec((m, tc), lambda j: (0, off // tc + j))

    def colb(off):
        return pl.BlockSpec((m, tb), lambda j: (0, off // tb + j))

    return pl.pallas_call(
        functools.partial(_mix_sample_kernel, nseq=nseq, seq=seq),
        grid=(nj,),
        in_specs=[cola(offs["x"]), cola(offs["g"]),
                  pl.BlockSpec((nseq, tc), lambda j: (0, j)),
                  pl.BlockSpec((ha, tc), lambda j: (0, j)),
                  pl.BlockSpec((CONV4_W, tc), lambda j: (0, j)),
                  vec, blk, blk, vec, vec, vec,
                  colb(offs["sb"]), colb(offs["sc"]), colb(offs["sx"]),
                  pl.BlockSpec((hb, tb), lambda j: (0, j)),
                  pl.BlockSpec((SC_W, tb), lambda j: (0, j))],
        out_specs=[pl.BlockSpec((m, tc), lambda j: (0, j)),
                   pl.BlockSpec((nseq, tc), lambda j: (0, j)),
                   pl.BlockSpec((ha, tc), lambda j: (0, j)),
                   pl.BlockSpec((m, tb), lambda j: (0, j)),
                   pl.BlockSpec((hb, tb), lambda j: (0, j))],
        out_shape=[jax.ShapeDtypeStruct((m, da), BF16),
                   jax.ShapeDtypeStruct((nseq, da), F32),
                   jax.ShapeDtypeStruct((ha, da), F32),
                   jax.ShapeDtypeStruct((m, db), BF16),
                   jax.ShapeDtypeStruct((hb, db), F32)],
        scratch_shapes=[pltpu.VMEM((ha + m, tc), F32),
                        pltpu.VMEM((hb + m, tb), F32)],
        compiler_params=_params(("parallel",), vm),
        name="mix_sample",
    )(z, z, h0, c4, w4, b4, wr, wi, br, bi, lam, z, z, z, buf, wsc)


def _time_major(s):
    b, w, c = s.shape
    return jnp.swapaxes(s, 0, 1).reshape(w * b, c)


def _batch_major(s, b):
    return jnp.swapaxes(s.reshape(-1, b, s.shape[1]), 0, 1)


def _layer(x, h, l, lw, gnext, group):
    d_rnn = lw["w_conv4"].shape[1]
    d_conv = lw["w_sc"].shape[1]
    off_x, off_g = 0, d_rnn
    off_sb = 2 * d_rnn
    off_sc = off_sb + d_conv
    off_sx = off_sc + d_conv
    off_ga = off_sx + d_conv
    off_gb = off_ga + x.shape[1]
    nb, seq = group["batch"], group["seq"]

    offs = {"x": off_x, "g": off_g, "sb": off_sb, "sc": off_sc, "sx": off_sx}
    z = _mm(h, lw["w_in"], l, F32, False, "mm_in")
    mixw = (l, lw["w_conv4"], lw["b_conv4"], lw["w_rg_r"], lw["w_rg_i"], lw["b_rg_r"],
            lw["b_rg_i"], lw["rg_lambda"], lw["w_sc"])
    if group["kind"] == "prompt":
        ua, hn, cn, ub, sn = _mix_prompt(z, offs, *mixw, nb, seq)
        hn = hn.reshape(nb, d_rnn)
    else:
        ua, hn, cn, ub, sn = _mix_sample(z, offs, group["h0"], _time_major(group["c4"]),
                                         _time_major(group["sc"]), *mixw, nb, seq)
        cn = _batch_major(cn, nb)
        sn = _batch_major(sn, nb)
    pp = group["kind"] == "prompt"
    mm_norm = _mm_norm_pp if pp else _mm_norm
    if pp:
        m, wob = _merge(ua, ub, lw["w_out_a"], lw["w_out_b"], l, z, off_ga, off_gb,
                        wo=lw["w_o"])
    else:
        (m,) = _merge(ua, ub, lw["w_out_a"], lw["w_out_b"], l, z, off_ga, off_gb)
        wob = group["wob"]
    x1, hm = mm_norm(m, wob, 0, x, lw["norm_mix_post"], lw["norm_mlp_pre"], "mm_o", ksub=2)
    a = _mm(hm, lw["w_mlp_up"], l, BF16, True, "mm_up", tiled_out=True)
    x2, hnext = mm_norm(a, lw["w_mlp_down"], l, x1, lw["norm_mlp_post"], gnext, "mm_down")
    return x2, hnext, hn, cn, sn, wob


def kernel(x_prompt, x_sample, state_rglru_h, state_conv4, state_shortconv, norm_mix_pre,
           norm_mix_post, w_in, w_conv4, b_conv4, w_rg_r, b_rg_r, w_rg_i, b_rg_i, rg_lambda,
           w_out_a, w_sc, w_out_b, w_o, norm_mlp_pre, norm_mlp_post, w_mlp_up, w_mlp_down):
    depth = w_in.shape[0]
    bp, sp, d = x_prompt.shape
    bs, ss, _ = x_sample.shape
    assert sp % (SUBLANES * SUBLANES) == 0 and bs % SUBLANES == 0

    xp = jnp.swapaxes(x_prompt.reshape(bp, SUBLANES, sp // SUBLANES, d), 1, 2)
    xp = xp.reshape(bp * sp, d)
    xs = jnp.swapaxes(x_sample, 0, 1).reshape(ss * bs, d)
    hp = _norm(xp, norm_mix_pre[0].reshape(1, d))
    hs = _norm(xs, norm_mix_pre[0].reshape(1, d))

    outs_p, outs_s = [], []
    for l in range(depth):
        lw = {
            "w_in": w_in,
            "w_conv4": w_conv4[l],
            "b_conv4": b_conv4[l].reshape(1, -1),
            "w_rg_r": w_rg_r,
            "w_rg_i": w_rg_i,
            "b_rg_r": b_rg_r[l].reshape(1, -1),
            "b_rg_i": b_rg_i[l].reshape(1, -1),
            "rg_lambda": rg_lambda[l].reshape(1, -1),
            "w_out_a": w_out_a,
            "w_sc": w_sc[l],
            "w_out_b": w_out_b,
            "w_o": w_o,
            "norm_mix_post": norm_mix_post[l].reshape(1, -1),
            "norm_mlp_pre": norm_mlp_pre[l].reshape(1, -1),
            "norm_mlp_post": norm_mlp_post[l].reshape(1, -1),
            "w_mlp_up": w_mlp_up,
            "w_mlp_down": w_mlp_down,
        }
        gnext = norm_mix_pre[(l + 1) % depth].reshape(1, -1)
        gp = {"kind": "prompt", "batch": bp, "seq": sp}
        gs = {"kind": "sample", "batch": bs, "seq": ss, "h0": state_rglru_h[l],
              "c4": state_conv4[l], "sc": state_shortconv[l]}
        xp, hp, hnp, cnp, snp, gs["wob"] = _layer(xp, hp, l, lw, gnext, gp)
        xs, hs, hns, cns, sns, _ = _layer(xs, hs, l, lw, gnext, gs)
        outs_p.append((hnp, cnp, snp))
        outs_s.append((hns, cns, sns))

    yp = jnp.swapaxes(xp.reshape(bp, sp // SUBLANES, SUBLANES, d), 1, 2).reshape(bp, sp, d)
    ys = jnp.swapaxes(xs.reshape(ss, bs, d), 0, 1)
    return (yp, ys,
            jnp.stack([o[0] for o in outs_p]), jnp.stack([o[1] for o in outs_p]),
            jnp.stack([o[2] for o in outs_p]),
            jnp.stack([o[0] for o in outs_s]), jnp.stack([o[1] for o in outs_s]),
            jnp.stack([o[2] for o in outs_s]))
```

```python
import functools

import jax
import jax.numpy as jnp
from jax import lax
from jax.experimental import pallas as pl
from jax.experimental.pallas import tpu as pltpu

RG_BLK = 256
RG_C = 8.0
CONV4_W = 4
SC_W = 3
EPS = 1e-6

SUBLANES = 8
V7X_VMEM_BUDGET = 58 * 1024 * 1024

F32 = jnp.float32
BF16 = jnp.bfloat16


def _params(semantics, vmem_bytes):
    return pltpu.CompilerParams(
        dimension_semantics=semantics,
        vmem_limit_bytes=min(int(vmem_bytes) + (12 << 20), V7X_VMEM_BUDGET),
    )


def _rms(x, g):
    ms = jnp.mean(x * x, axis=-1, keepdims=True)
    return (x * lax.rsqrt(ms + EPS)) * g


def _sigmoid(x):
    return 0.5 * jnp.tanh(0.5 * x) + 0.5


GELU_C = 0.7978845608028654
GELU_C3 = GELU_C * 0.044715


def _gelu(x):
    hx = 0.5 * x
    return hx * jnp.tanh(x * (GELU_C + GELU_C3 * (x * x))) + hx


def _norm_kernel(x_ref, g_ref, o_ref):
    o_ref[...] = _rms(x_ref[...], g_ref[...]).astype(o_ref.dtype)


def _norm(x, g, tm=256):
    m, d = x.shape
    tm = min(tm, m)
    return pl.pallas_call(
        _norm_kernel,
        grid=(m // tm,),
        in_specs=[pl.BlockSpec((tm, d), lambda i: (i, 0)),
                  pl.BlockSpec((1, d), lambda i: (0, 0))],
        out_specs=pl.BlockSpec((tm, d), lambda i: (i, 0)),
        out_shape=jax.ShapeDtypeStruct((m, d), BF16),
        compiler_params=_params(("parallel",), 2 * tm * d * 6),
        name="norm0",
    )(x, g)


def _mm_kernel(x_ref, w_ref, o_ref, *, relu2):
    acc = jnp.dot(x_ref[...], w_ref[...].astype(BF16), preferred_element_type=F32)
    if relu2:
        acc = jnp.square(jnp.maximum(acc, 0.0))
    o_ref[...] = acc.astype(o_ref.dtype)


def _mm(x, w, l, out_dtype, relu2, name, tm=2048, tn=512, tiled_out=False):
    m, k = x.shape
    n = w.shape[2]
    tm, tn = min(tm, m), min(tn, n)
    if tiled_out:
        out_spec = pl.BlockSpec((None, tm, tn), lambda i, j: (j, i, 0))
        out_shape = jax.ShapeDtypeStruct((n // tn, m, tn), out_dtype)
    else:
        out_spec = pl.BlockSpec((tm, tn), lambda i, j: (i, j))
        out_shape = jax.ShapeDtypeStruct((m, n), out_dtype)
    vm = (tm * k * 2 + 2 * (k * tn * 4 + tm * tn * jnp.dtype(out_dtype).itemsize)
          + k * tn * 2 + tm * tn * 4)
    return pl.pallas_call(
        functools.partial(_mm_kernel, relu2=relu2),
        grid=(m // tm, n // tn),
        in_specs=[pl.BlockSpec((tm, k), lambda i, j: (i, 0), pipeline_mode=pl.Buffered(1)),
                  pl.BlockSpec((None, k, tn), lambda i, j: (l, 0, j))],
        out_specs=out_spec,
        out_shape=out_shape,
        compiler_params=_params(("parallel", "arbitrary"), vm),
        name=name,
    )(x, w)


def _merge_kernel(ua_ref, ub_ref, wa_ref, wb_ref, ga_ref, gb_ref, *rest):
    o_ref = rest[-1] if len(rest) == 1 else rest[1]
    ya = jnp.dot(ua_ref[...], wa_ref[...].astype(BF16), preferred_element_type=F32)
    yb = jnp.dot(ub_ref[...], wb_ref[...].astype(BF16), preferred_element_type=F32)
    m = _sigmoid(ga_ref[...]) * ya + _sigmoid(gb_ref[...]) * yb
    o_ref[...] = m.astype(o_ref.dtype)
    if len(rest) == 3:
        wo_ref, _, wob_ref = rest
        wob_ref[...] = wo_ref[...].astype(wob_ref.dtype)


def _merge(ua, ub, wa, wb, l, z, off_ga, off_gb, wo=None, tm=1024, tn=512):
    m, ka = ua.shape
    kb = ub.shape[1]
    n = wa.shape[2]
    tm, tn = min(tm, m), min(tn, n)
    ni, nj = m // tm, n // tn
    vm = (2 * (tm * ka * 2 + tm * kb * 2 + ka * tn * 2 + kb * tn * 2
               + 2 * tm * tn * 4 + tm * tn * 2)
          + 2 * tm * tn * 4)
    in_specs = [pl.BlockSpec((tm, ka), lambda i, j: (i, 0)),
                pl.BlockSpec((tm, kb), lambda i, j: (i, 0)),
                pl.BlockSpec((None, ka, tn), lambda i, j: (0, 0, j)),
                pl.BlockSpec((None, kb, tn), lambda i, j: (0, 0, j)),
                pl.BlockSpec((tm, tn), lambda i, j: (i, off_ga // tn + j)),
                pl.BlockSpec((tm, tn), lambda i, j: (i, off_gb // tn + j))]
    out_specs = [pl.BlockSpec((None, tm, tn), lambda i, j: (j, i, 0))]
    out_shape = [jax.ShapeDtypeStruct((nj, m, tn), BF16)]
    args = [ua, ub, wa, wb, z, z]
    if wo is not None:
        kw, nw = wo.shape[1], wo.shape[2]
        rows = kw // (ni * nj)
        assert rows % 16 == 0
        in_specs.append(pl.BlockSpec((None, rows, nw), lambda i, j: (l, i * nj + j, 0)))
        out_specs.append(pl.BlockSpec((None, rows, nw), lambda i, j: (0, i * nj + j, 0)))
        out_shape.append(jax.ShapeDtypeStruct((1, kw, nw), BF16))
        args.append(wo)
        vm += 2 * rows * nw * 6
    return pl.pallas_call(
        _merge_kernel,
        grid=(ni, nj),
        in_specs=in_specs,
        out_specs=out_specs,
        out_shape=out_shape,
        compiler_params=_params(("arbitrary", "arbitrary"), vm),
        name="merge",
    )(*args)


def _tiles_dot(a_ref, w_ref):
    ksub = a_ref.shape[0]
    a = a_ref[0] if ksub == 1 else jnp.concatenate([a_ref[s] for s in range(ksub)], axis=1)
    return jnp.dot(a, w_ref[...].astype(BF16), preferred_element_type=F32)


def _mm_norm_kernel(a_ref, w_ref, x_ref, gpost_ref, gnext_ref, xo_ref, ho_ref, acc_ref,
                    *, nk, te):
    k = pl.program_id(1)

    @pl.when(k == 0)
    def _():
        acc_ref[...] = _tiles_dot(a_ref, w_ref)

    @pl.when(jnp.logical_and(k > 0, k < nk))
    def _():
        acc_ref[...] += _tiles_dot(a_ref, w_ref)

    @pl.when(k >= nk)
    def _():
        r0 = pl.multiple_of((k - nk) * te, te)
        o = acc_ref[pl.ds(r0, te), :]
        x1 = x_ref[...] + _rms(o, gpost_ref[...])
        xo_ref[...] = x1
        ho_ref[...] = _rms(x1, gnext_ref[...]).astype(ho_ref.dtype)


def _mm_norm(a, w, l, x, gpost, gnext, name, tm=1024, te=128, ksub=1):
    nka, m, tka = a.shape
    nk, tk = nka // ksub, ksub * tka
    d = w.shape[2]
    tm = min(tm, m)
    te = min(te, tm)
    ne = tm // te
    vm = (2 * (tm * tk * 2 + tk * d * 4 + te * d * 4 + te * d * 4 + te * d * 2)
          + tm * d * 4 + tk * d * 2)

    def slab(i, k):
        return (i * ne + jnp.maximum(k - nk, 0), 0)

    return pl.pallas_call(
        functools.partial(_mm_norm_kernel, nk=nk, te=te),
        grid=(m // tm, nk + ne),
        in_specs=[pl.BlockSpec((ksub, tm, tka), lambda i, k: (jnp.minimum(k, nk - 1), i, 0)),
                  pl.BlockSpec((None, tk, d), lambda i, k: (l, jnp.minimum(k, nk - 1), 0)),
                  pl.BlockSpec((te, d), slab),
                  pl.BlockSpec((1, d), lambda i, k: (0, 0)),
                  pl.BlockSpec((1, d), lambda i, k: (0, 0))],
        out_specs=[pl.BlockSpec((te, d), slab),
                   pl.BlockSpec((te, d), slab)],
        out_shape=[jax.ShapeDtypeStruct((m, d), F32),
                   jax.ShapeDtypeStruct((m, d), BF16)],
        scratch_shapes=[pltpu.VMEM((tm, d), F32)],
        compiler_params=_params(("parallel", "arbitrary"), vm),
        name=name,
    )(a, w, x, gpost, gnext)


def _mm_norm_pp_kernel(a_ref, w_ref, x_ref, gpost_ref, gnext_ref, xo_ref, ho_ref,
                       acc0_ref, acc1_ref, *, ni, te):
    p = pl.program_id(0)
    k = pl.program_id(1)
    accs = (acc0_ref, acc1_ref)

    def contract(acc_ref):
        acc_ref[...] += _tiles_dot(a_ref, w_ref)

    def drain(acc_ref):
        r0 = pl.multiple_of(k * te, te)
        o = acc_ref[pl.ds(r0, te), :]
        acc_ref[pl.ds(r0, te), :] = jnp.zeros_like(o)
        x1 = x_ref[...] + _rms(o, gpost_ref[...])
        xo_ref[...] = x1
        ho_ref[...] = _rms(x1, gnext_ref[...]).astype(ho_ref.dtype)

    @pl.when(p == 0)
    def _():
        @pl.when(k == 0)
        def _():
            acc0_ref[...] = jnp.zeros(acc0_ref.shape, F32)
            acc1_ref[...] = jnp.zeros(acc1_ref.shape, F32)
        contract(acc0_ref)

    for parity in (0, 1):
        @pl.when(jnp.logical_and(jnp.logical_and(p > 0, p < ni), p % 2 == parity))
        def _():
            contract(accs[parity])
            drain(accs[1 - parity])

    @pl.when(p == ni)
    def _():
        drain(accs[(ni - 1) % 2])


def _mm_norm_pp(a, w, l, x, gpost, gnext, name, tm=1024, ksub=1):
    nka, m, tka = a.shape
    nk, tk = nka // ksub, ksub * tka
    d = w.shape[2]
    tm = min(tm, m // 2)
    ni = m // tm
    te = tm // nk
    assert te % 16 == 0
    vm = (2 * (tm * tk * 2 + tk * d * 4 + te * d * 4 + te * d * 4 + te * d * 2)
          + 2 * tm * d * 4 + tk * d * 2)

    def kk(p, k):
        return jnp.where(p < ni, k, nk - 1)

    def slab(p, k):
        return (jnp.where(p > 0, (p - 1) * nk + k, 0), 0)

    return pl.pallas_call(
        functools.partial(_mm_norm_pp_kernel, ni=ni, te=te),
        grid=(ni + 1, nk),
        in_specs=[pl.BlockSpec((ksub, tm, tka),
                               lambda p, k: (kk(p, k), jnp.minimum(p, ni - 1), 0)),
                  pl.BlockSpec((None, tk, d), lambda p, k: (l, kk(p, k), 0)),
                  pl.BlockSpec((te, d), slab),
                  pl.BlockSpec((1, d), lambda p, k: (0, 0)),
                  pl.BlockSpec((1, d), lambda p, k: (0, 0))],
        out_specs=[pl.BlockSpec((te, d), slab),
                   pl.BlockSpec((te, d), slab)],
        out_shape=[jax.ShapeDtypeStruct((m, d), F32),
                   jax.ShapeDtypeStruct((m, d), BF16)],
        scratch_shapes=[pltpu.VMEM((tm, d), F32), pltpu.VMEM((tm, d), F32)],
        compiler_params=_params(("arbitrary", "arbitrary"), vm),
        name=name,
    )(a, w, x, gpost, gnext)


def _softplus_neg(lam):
    return jnp.maximum(-lam, 0.0) + jnp.log1p(jnp.exp(-jnp.abs(lam)))


def _rglru_coeffs(xc, wr_ref, wi_ref, br, bi, sp):
    xcb = xc.astype(BF16)
    hbr, hbi = 0.5 * br, 0.5 * bi
    c = (-0.5 * RG_C) * sp
    a_parts, b_parts = [], []
    for g in range(xc.shape[1] // RG_BLK):
        sl = slice(g * RG_BLK, (g + 1) * RG_BLK)
        xg = xcb[:, sl]
        tr = jnp.tanh(jnp.dot(xg, (0.5 * wr_ref[g]).astype(BF16),
                              preferred_element_type=F32) + hbr[:, sl])
        ti = jnp.tanh(jnp.dot(xg, (0.5 * wi_ref[g]).astype(BF16),
                              preferred_element_type=F32) + hbi[:, sl])
        log_a = c[:, sl] * tr + c[:, sl]
        a = jnp.exp(log_a)
        y = jnp.tanh(log_a) * (-1.0 - a * a)
        mult = jnp.where(y > 0.0, y * lax.rsqrt(y), 0.0)
        a_parts.append(a)
        b_parts.append((mult * xc[:, sl]) * (0.5 * ti + 0.5))
    if len(a_parts) == 1:
        return a_parts[0], b_parts[0]
    return jnp.concatenate(a_parts, axis=1), jnp.concatenate(b_parts, axis=1)


def _conv_taps(taps, w, width):
    acc = taps[0] * w[0:1, :]
    for k in range(1, width):
        acc = acc + taps[k] * w[k:k + 1, :]
    return acc


def _scan8(a, b):
    row = lax.broadcasted_iota(jnp.int32, a.shape, 0)
    for s in (1, 2, 4):
        keep = row >= s
        b = jnp.where(keep, a * pltpu.roll(b, s, 0) + b, b)
        a = jnp.where(keep, a * pltpu.roll(a, s, 0), a)
    return a, b


def _prev_segment(grp):
    row = lax.broadcasted_iota(jnp.int32, grp.shape, 0)
    return jnp.where(row == 0, 0.0, pltpu.roll(grp, 1, 0))


def _mix_a_prompt_kernel(x_ref, g_ref, w4_ref, b4_ref, wr_ref, wi_ref, br_ref, bi_ref,
                         lam_ref, u_ref, hn_ref, cn_ref, xh_ref, hh_ref, pp_ref,
                         *, seq, rc):
    tc = x_ref.shape[1]
    halo = (CONV4_W - 1) * SUBLANES
    w4 = w4_ref[...]
    b4 = b4_ref[...]
    br = br_ref[...]
    bi = bi_ref[...]
    sp = _softplus_neg(lam_ref[...])

    for k in range(CONV4_W - 1):
        grp = x_ref[seq - halo + SUBLANES * k:seq - halo + SUBLANES * (k + 1), :]
        xh_ref[SUBLANES * k:SUBLANES * (k + 1), :] = _prev_segment(grp)

    h = jnp.zeros((SUBLANES, tc), F32)
    p = jnp.ones((SUBLANES, tc), F32)
    for c in range(seq // rc):
        r0 = c * rc
        if c == 0:
            xh_ref[halo:halo + rc, :] = x_ref[0:rc, :]
            taps = [xh_ref[SUBLANES * k:SUBLANES * k + rc, :] for k in range(CONV4_W)]
        else:
            taps = [x_ref[r0 - halo + SUBLANES * k:r0 - halo + SUBLANES * k + rc, :]
                    for k in range(CONV4_W)]
        xc = _conv_taps(taps, w4, CONV4_W) + b4
        a, b = _rglru_coeffs(xc, wr_ref, wi_ref, br, bi, sp)
        for q in range(rc // SUBLANES):
            aq = a[SUBLANES * q:SUBLANES * (q + 1), :]
            h = aq * h + b[SUBLANES * q:SUBLANES * (q + 1), :]
            p = aq * p
            hh_ref[r0 + SUBLANES * q:r0 + SUBLANES * (q + 1), :] = h
            pp_ref[r0 + SUBLANES * q:r0 + SUBLANES * (q + 1), :] = p

    _, hc = _scan8(p, h)
    cin = _prev_segment(hc)
    hn_ref[0] = hc[SUBLANES - 1:SUBLANES, :]

    for c in range(seq // rc):
        r0 = c * rc
        cin_t = jnp.tile(cin, (rc // SUBLANES, 1))
        hfull = hh_ref[r0:r0 + rc, :] + pp_ref[r0:r0 + rc, :] * cin_t
        u_ref[r0:r0 + rc, :] = (_gelu(g_ref[r0:r0 + rc, :]) * hfull).astype(u_ref.dtype)

    for k in range(CONV4_W - 1):
        r = seq - halo + SUBLANES * k + SUBLANES - 1
        cn_ref[0, k:k + 1, :] = x_ref[r:r + 1, :]


def _mix_a_sample_kernel(x_ref, g_ref, h0_ref, c4_ref, w4_ref, b4_ref, wr_ref, wi_ref,
                         br_ref, bi_ref, lam_ref, u_ref, hn_ref, cn_ref, xh_ref,
                         *, nseq, seq):
    halo = (CONV4_W - 1) * nseq
    m = nseq * seq
    sp = _softplus_neg(lam_ref[...])
    xh_ref[0:halo, :] = c4_ref[...]
    xh_ref[halo:halo + m, :] = x_ref[...]
    taps = [xh_ref[nseq * k:nseq * k + m, :] for k in range(CONV4_W)]
    xc = _conv_taps(taps, w4_ref[...], CONV4_W) + b4_ref[...]
    a, b = _rglru_coeffs(xc, wr_ref, wi_ref, br_ref[...], bi_ref[...], sp)
    h = h0_ref[...]
    for t in range(seq):
        rows = slice(nseq * t, nseq * (t + 1))
        h = a[rows, :] * h + b[rows, :]
        u_ref[rows, :] = (_gelu(g_ref[rows, :]) * h).astype(u_ref.dtype)
    hn_ref[...] = h
    cn_ref[...] = xh_ref[m:m + halo, :]


def _mix_b_prompt_kernel(sb_ref, sc_ref, sx_ref, w_ref, u_ref, sn_ref, uh_ref, *, seq, rc):
    halo = (SC_W - 1) * SUBLANES
    w = w_ref[...]
    for c in range(seq // rc):
        r0 = c * rc
        uh_ref[halo + r0:halo + r0 + rc, :] = sc_ref[r0:r0 + rc, :] * sx_ref[r0:r0 + rc, :]
    for k in range(SC_W - 1):
        grp = uh_ref[seq + SUBLANES * k:seq + SUBLANES * (k + 1), :]
        uh_ref[SUBLANES * k:SUBLANES * (k + 1), :] = _prev_segment(grp)
    for c in range(seq // rc):
        r0 = c * rc
        taps = [uh_ref[r0 + SUBLANES * k:r0 + SUBLANES * k + rc, :] for k in range(SC_W)]
        uc = _conv_taps(taps, w, SC_W)
        u_ref[r0:r0 + rc, :] = (sb_ref[r0:r0 + rc, :] * uc).astype(u_ref.dtype)
    for k in range(SC_W - 1):
        r = seq + SUBLANES * k + SUBLANES - 1
        sn_ref[0, k:k + 1, :] = uh_ref[r:r + 1, :]


def _mix_b_sample_kernel(sb_ref, sc_ref, sx_ref, buf_ref, w_ref, u_ref, sn_ref, uh_ref,
                         *, nseq, seq):
    halo = (SC_W - 1) * nseq
    m = nseq * seq
    uh_ref[0:halo, :] = buf_ref[...]
    uh_ref[halo:halo + m, :] = sc_ref[...] * sx_ref[...]
    taps = [uh_ref[nseq * k:nseq * k + m, :] for k in range(SC_W)]
    uc = _conv_taps(taps, w_ref[...], SC_W)
    u_ref[...] = (sb_ref[...] * uc).astype(u_ref.dtype)
    sn_ref[...] = uh_ref[m:m + halo, :]


def _mix_prompt_kernel(x_ref, g_ref, w4_ref, b4_ref, wr_ref, wi_ref, br_ref, bi_ref, lam_ref,
                       sb_ref, sc_ref, sx_ref, wsc_ref, wa_ref, wb_ref,
                       ua_ref, hn_ref, cn_ref, ub_ref, sn_ref, wab_ref, wbb_ref,
                       xh_ref, hh_ref, pp_ref, uh_ref, *, seq, rc):
    wab_ref[...] = wa_ref[...].astype(wab_ref.dtype)
    wbb_ref[...] = wb_ref[...].astype(wbb_ref.dtype)
    _mix_b_prompt_kernel(sb_ref, sc_ref, sx_ref, wsc_ref, ub_ref, sn_ref, uh_ref,
                         seq=seq, rc=rc)
    _mix_a_prompt_kernel(x_ref, g_ref, w4_ref, b4_ref, wr_ref, wi_ref, br_ref, bi_ref,
                         lam_ref, ua_ref, hn_ref, cn_ref, xh_ref, hh_ref, pp_ref,
                         seq=seq, rc=rc)


def _mix_prompt(z, offs, l, w4, b4, wr, wi, br, bi, lam, wsc, wa, wb, batch, seq,
                tc=256, rc=256):
    da, db = w4.shape[1], wsc.shape[1]
    tc, rc = min(tc, da), min(rc, seq)
    nj = da // tc
    tb = db // nj
    dm = wa.shape[2]
    ra, rb = wa.shape[1] // (nj * batch), wb.shape[1] // (nj * batch)
    assert ra % 16 == 0 and rb % 16 == 0

    def wslab(rows, layer):
        return pl.BlockSpec((None, rows, dm), lambda j, b: (layer, j * batch + b, 0))
    vec = pl.BlockSpec((1, tc), lambda j, b: (0, j))
    blk = pl.BlockSpec((None, tc // RG_BLK, RG_BLK, RG_BLK), lambda j, b: (l, j, 0, 0))
    vm = (2 * (2 * seq * tc * 4 + seq * tc * 2 + 2 * tc * RG_BLK * 4 + 3 * seq * tb * 4
               + seq * tb * 2) + 3 * seq * tc * 4 + (seq + 16) * tb * 4)

    def cola(off):
        return pl.BlockSpec((seq, tc), lambda j, b: (b, off // tc + j))

    def colb(off):
        return pl.BlockSpec((seq, tb), lambda j, b: (b, off // tb + j))

    return pl.pallas_call(
        functools.partial(_mix_prompt_kernel, seq=seq, rc=rc),
        grid=(nj, batch),
        in_specs=[cola(offs["x"]), cola(offs["g"]),
                  pl.BlockSpec((CONV4_W, tc), lambda j, b: (0, j)),
                  vec, blk, blk, vec, vec, vec,
                  colb(offs["sb"]), colb(offs["sc"]), colb(offs["sx"]),
                  pl.BlockSpec((SC_W, tb), lambda j, b: (0, j)),
                  wslab(ra, l), wslab(rb, l)],
        out_specs=[pl.BlockSpec((seq, tc), lambda j, b: (b, j)),
                   pl.BlockSpec((1, 1, tc), lambda j, b: (b, 0, j)),
                   pl.BlockSpec((1, CONV4_W - 1, tc), lambda j, b: (b, 0, j)),
                   pl.BlockSpec((seq, tb), lambda j, b: (b, j)),
                   pl.BlockSpec((1, SC_W - 1, tb), lambda j, b: (b, 0, j)),
                   wslab(ra, 0), wslab(rb, 0)],
        out_shape=[jax.ShapeDtypeStruct((batch * seq, da), BF16),
                   jax.ShapeDtypeStruct((batch, 1, da), F32),
                   jax.ShapeDtypeStruct((batch, CONV4_W - 1, da), F32),
                   jax.ShapeDtypeStruct((batch * seq, db), BF16),
                   jax.ShapeDtypeStruct((batch, SC_W - 1, db), F32),
                   jax.ShapeDtypeStruct((1,) + wa.shape[1:], BF16),
                   jax.ShapeDtypeStruct((1,) + wb.shape[1:], BF16)],
        scratch_shapes=[pltpu.VMEM(((CONV4_W - 1) * SUBLANES + rc, tc), F32),
                        pltpu.VMEM((seq, tc), F32),
                        pltpu.VMEM((seq, tc), F32),
                        pltpu.VMEM(((SC_W - 1) * SUBLANES + seq, tb), F32)],
        compiler_params=_params(("parallel", "arbitrary"), vm),
        name="mix_prompt",
    )(z, z, w4, b4, wr, wi, br, bi, lam, z, z, z, wsc, wa, wb)


def _mix_sample_kernel(x_ref, g_ref, h0_ref, c4_ref, w4_ref, b4_ref, wr_ref, wi_ref, br_ref,
                       bi_ref, lam_ref, sb_ref, sc_ref, sx_ref, buf_ref, wsc_ref,
                       ua_ref, hn_ref, cn_ref, ub_ref, sn_ref, xh_ref, uh_ref, *, nseq, seq):
    _mix_b_sample_kernel(sb_ref, sc_ref, sx_ref, buf_ref, wsc_ref, ub_ref, sn_ref, uh_ref,
                         nseq=nseq, seq=seq)
    _mix_a_sample_kernel(x_ref, g_ref, h0_ref, c4_ref, w4_ref, b4_ref, wr_ref, wi_ref,
                         br_ref, bi_ref, lam_ref, ua_ref, hn_ref, cn_ref, xh_ref,
                         nseq=nseq, seq=seq)


def _mix_sample(z, offs, h0, c4, buf, l, w4, b4, wr, wi, br, bi, lam, wsc, nseq, seq, tc=256):
    da, db = w4.shape[1], wsc.shape[1]
    tc = min(tc, da)
    nj = da // tc
    tb = db // nj
    m = nseq * seq
    ha, hb = (CONV4_W - 1) * nseq, (SC_W - 1) * nseq
    vec = pl.BlockSpec((1, tc), lambda j: (0, j))
    blk = pl.BlockSpec((None, tc // RG_BLK, RG_BLK, RG_BLK), lambda j: (l, j, 0, 0))
    vm = (2 * (2 * m * tc * 4 + m * tc * 2 + nseq * tc * 8 + 2 * ha * tc * 4
               + 2 * tc * RG_BLK * 4 + 3 * m * tb * 4 + m * tb * 2 + 2 * hb * tb * 4)
          + 8 * m * tc * 4 + 4 * m * tb * 4)

    def cola(off):
        return pl.BlockSpec((m, tc), lambda j: (0, off // tc + j))

    def colb(off):
        return pl.BlockSpec((m, tb), lambda j: (0, off // tb + j))

    return pl.pallas_call(
        functools.partial(_mix_sample_kernel, nseq=nseq, seq=seq),
        grid=(nj,),
        in_specs=[cola(offs["x"]), cola(offs["g"]),
                  pl.BlockSpec((nseq, tc), lambda j: (0, j)),
                  pl.BlockSpec((ha, tc), lambda j: (0, j)),
                  pl.BlockSpec((CONV4_W, tc), lambda j: (0, j)),
                  vec, blk, blk, vec, vec, vec,
                  colb(offs["sb"]), colb(offs["sc"]), colb(offs["sx"]),
                  pl.BlockSpec((hb, tb), lambda j: (0, j)),
                  pl.BlockSpec((SC_W, tb), lambda j: (0, j))],
        out_specs=[pl.BlockSpec((m, tc), lambda j: (0, j)),
                   pl.BlockSpec((nseq, tc), lambda j: (0, j)),
                   pl.BlockSpec((ha, tc), lambda j: (0, j)),
                   pl.BlockSpec((m, tb), lambda j: (0, j)),
                   pl.BlockSpec((hb, tb), lambda j: (0, j))],
        out_shape=[jax.ShapeDtypeStruct((m, da), BF16),
                   jax.ShapeDtypeStruct((nseq, da), F32),
                   jax.ShapeDtypeStruct((ha, da), F32),
                   jax.ShapeDtypeStruct((m, db), BF16),
                   jax.ShapeDtypeStruct((hb, db), F32)],
        scratch_shapes=[pltpu.VMEM((ha + m, tc), F32),
                        pltpu.VMEM((hb + m, tb), F32)],
        compiler_params=_params(("parallel",), vm),
        name="mix_sample",
    )(z, z, h0, c4, w4, b4, wr, wi, br, bi, lam, z, z, z, buf, wsc)


def _time_major(s):
    b, w, c = s.shape
    return jnp.swapaxes(s, 0, 1).reshape(w * b, c)


def _batch_major(s, b):
    return jnp.swapaxes(s.reshape(-1, b, s.shape[1]), 0, 1)


def _layer(x, h, l, lw, gnext, group):
    d_rnn = lw["w_conv4"].shape[1]
    d_conv = lw["w_sc"].shape[1]
    off_x, off_g = 0, d_rnn
    off_sb = 2 * d_rnn
    off_sc = off_sb + d_conv
    off_sx = off_sc + d_conv
    off_ga = off_sx + d_conv
    off_gb = off_ga + x.shape[1]
    nb, seq = group["batch"], group["seq"]

    offs = {"x": off_x, "g": off_g, "sb": off_sb, "sc": off_sc, "sx": off_sx}
    z = _mm(h, lw["w_in"], l, F32, False, "mm_in")
    mixw = (l, lw["w_conv4"], lw["b_conv4"], lw["w_rg_r"], lw["w_rg_i"], lw["b_rg_r"],
            lw["b_rg_i"], lw["rg_lambda"], lw["w_sc"])
    pp = group["kind"] == "prompt"
    if pp:
        ua, hn, cn, ub, sn, wab, wbb = _mix_prompt(z, offs, *mixw, lw["w_out_a"],
                                                   lw["w_out_b"], nb, seq)
        hn = hn.reshape(nb, d_rnn)
        m, wob = _merge(ua, ub, wab, wbb, l, z, off_ga, off_gb, wo=lw["w_o"])
    else:
        ua, hn, cn, ub, sn = _mix_sample(z, offs, group["h0"], _time_major(group["c4"]),
                                         _time_major(group["sc"]), *mixw, nb, seq)
        cn = _batch_major(cn, nb)
        sn = _batch_major(sn, nb)
        wab, wbb, wob = group["wbf"]
        (m,) = _merge(ua, ub, wab, wbb, l, z, off_ga, off_gb)
    mm_norm = _mm_norm_pp if pp else _mm_norm
    x1, hm = mm_norm(m, wob, 0, x, lw["norm_mix_post"], lw["norm_mlp_pre"], "mm_o")
    a = _mm(hm, lw["w_mlp_up"], l, BF16, True, "mm_up", tiled_out=True)
    x2, hnext = mm_norm(a, lw["w_mlp_down"], l, x1, lw["norm_mlp_post"], gnext, "mm_down")
    return x2, hnext, hn, cn, sn, (wab, wbb, wob)


def kernel(x_prompt, x_sample, state_rglru_h, state_conv4, state_shortconv, norm_mix_pre,
           norm_mix_post, w_in, w_conv4, b_conv4, w_rg_r, b_rg_r, w_rg_i, b_rg_i, rg_lambda,
           w_out_a, w_sc, w_out_b, w_o, norm_mlp_pre, norm_mlp_post, w_mlp_up, w_mlp_down):
    depth = w_in.shape[0]
    bp, sp, d = x_prompt.shape
    bs, ss, _ = x_sample.shape
    assert sp % (SUBLANES * SUBLANES) == 0 and bs % SUBLANES == 0

    xp = jnp.swapaxes(x_prompt.reshape(bp, SUBLANES, sp // SUBLANES, d), 1, 2)
    xp = xp.reshape(bp * sp, d)
    xs = jnp.swapaxes(x_sample, 0, 1).reshape(ss * bs, d)
    hp = _norm(xp, norm_mix_pre[0].reshape(1, d))
    hs = _norm(xs, norm_mix_pre[0].reshape(1, d))

    outs_p, outs_s = [], []
    for l in range(depth):
        lw = {
            "w_in": w_in,
            "w_conv4": w_conv4[l],
            "b_conv4": b_conv4[l].reshape(1, -1),
            "w_rg_r": w_rg_r,
            "w_rg_i": w_rg_i,
            "b_rg_r": b_rg_r[l].reshape(1, -1),
            "b_rg_i": b_rg_i[l].reshape(1, -1),
            "rg_lambda": rg_lambda[l].reshape(1, -1),
            "w_out_a": w_out_a,
            "w_sc": w_sc[l],
            "w_out_b": w_out_b,
            "w_o": w_o,
            "norm_mix_post": norm_mix_post[l].reshape(1, -1),
            "norm_mlp_pre": norm_mlp_pre[l].reshape(1, -1),
            "norm_mlp_post": norm_mlp_post[l].reshape(1, -1),
            "w_mlp_up": w_mlp_up,
            "w_mlp_down": w_mlp_down,
        }
        gnext = norm_mix_pre[(l + 1) % depth].reshape(1, -1)
        gp = {"kind": "prompt", "batch": bp, "seq": sp}
        gs = {"kind": "sample", "batch": bs, "seq": ss, "h0": state_rglru_h[l],
              "c4": state_conv4[l], "sc": state_shortconv[l]}
        xp, hp, hnp, cnp, snp, gs["wbf"] = _layer(xp, hp, l, lw, gnext, gp)
        xs, hs, hns, cns, sns, _ = _layer(xs, hs, l, lw, gnext, gs)
        outs_p.append((hnp, cnp, snp))
        outs_s.append((hns, cns, sns))

    yp = jnp.swapaxes(xp.reshape(bp, sp // SUBLANES, SUBLANES, d), 1, 2).reshape(bp, sp, d)
    ys = jnp.swapaxes(xs.reshape(ss, bs, d), 0, 1)
    return (yp, ys,
            jnp.stack([o[0] for o in outs_p]), jnp.stack([o[1] for o in outs_p]),
            jnp.stack([o[2] for o in outs_p]),
            jnp.stack([o[0] for o in outs_s]), jnp.stack([o[1] for o in outs_s]),
            jnp.stack([o[2] for o in outs_s]))
```
